```python
import math
import jax, jax.numpy as jnp
from jax import lax
import numpy as np

D_MODEL = 2048
BATCH = 8
SEQ = 2048
DEPTH = 1
DEC_BATCH = 32
DEC_SEQ = 8
PAST_LEN = 16384
PAGE_SIZE = 128

D_MIX = D_MODEL
D_ATTN = D_MIX // 2
D_CONV = D_MIX - D_ATTN
ATTN_HEADS = 8
QK_DIM = D_ATTN // (2 * ATTN_HEADS)
N_QK_HEADS = 2 * ATTN_HEADS
V_DIM = 2 * QK_DIM
ROT_DIM = QK_DIM // 4
ROPE_THETA = 500000.0
CONV_WIDTH = 31
Q_BLOCK = 128
SCALE = QK_DIM ** -0.5
SPLIT_POINTS = (D_ATTN, 2 * D_ATTN, 3 * D_ATTN, 4 * D_ATTN, 4 * D_ATTN + 2 * D_CONV)
D_IN = 4 * D_ATTN + 3 * D_CONV

kernel_name = 'hymba_diffattn_conformer_step'


def _rms(x, g, eps=1e-6):
    xf = x.astype(jnp.float32)
    y = xf * lax.rsqrt(jnp.mean(xf * xf, axis=-1, keepdims=True) + eps)
    return (y * g.astype(jnp.float32)).astype(x.dtype)


def _layernorm(x, g, b, eps=1e-5):
    xf = x.astype(jnp.float32)
    mu = jnp.mean(xf, axis=-1, keepdims=True)
    var = jnp.mean(jnp.square(xf - mu), axis=-1, keepdims=True)
    y = (xf - mu) * lax.rsqrt(var + eps) * g.astype(jnp.float32) + b.astype(jnp.float32)
    return y.astype(x.dtype)


def _rope(x, pos):
    half = ROT_DIM // 2
    inv = jnp.power(ROPE_THETA, -jnp.arange(half, dtype=jnp.float32) * (2.0 / ROT_DIM))
    ang = pos.astype(jnp.float32)[:, None] * inv[None, :]
    cos = jnp.cos(ang)[None, :, None, :]
    sin = jnp.sin(ang)[None, :, None, :]
    xf = x.astype(jnp.float32)
    x1 = xf[..., :half]
    x2 = xf[..., half:ROT_DIM]
    out = jnp.concatenate([x1 * cos - x2 * sin, x2 * cos + x1 * sin, xf[..., ROT_DIM:]], axis=-1)
    return out.astype(x.dtype)


def _diff_weights(s, mask, lam):
    p = jax.nn.softmax(jnp.where(mask, s, -jnp.inf), axis=-1)
    p = p.reshape(p.shape[:-3] + (ATTN_HEADS, 2) + p.shape[-2:])
    return p[..., 0, :, :] - lam * p[..., 1, :, :]


def _prompt_attn(q, k, v, lam):
    B, T = q.shape[:2]
    kpos = jnp.arange(T)

    def block(i):
        start = i * Q_BLOCK
        qb = lax.dynamic_slice_in_dim(q, start, Q_BLOCK, axis=1)
        s = jnp.einsum('bqhd,bkhd->bhqk', qb, k, preferred_element_type=jnp.float32) * SCALE
        qpos = start + jnp.arange(Q_BLOCK)
        mask = kpos[None, :] <= qpos[:, None]
        a = _diff_weights(s, mask, lam)
        return jnp.einsum('bhqk,bkhd->bqhd', a, v.astype(jnp.float32)).astype(v.dtype)

    o = lax.map(block, jnp.arange(T // Q_BLOCK))
    return o.transpose(1, 0, 2, 3, 4).reshape(B, T, ATTN_HEADS, V_DIM)


def _sample_attn(q, k_new, v_new, k_cache, v_cache, page_table):
    past = page_table.shape[1] * PAGE_SIZE
    S = q.shape[1]
    kidx = jnp.arange(past + S)
    qidx = past + jnp.arange(S)
    mask = kidx[None, :] <= qidx[:, None]

    def one(args):
        qs, ks, vs, pages, lam = args
        kp = k_cache[pages].reshape(past, N_QK_HEADS, QK_DIM)
        vp = v_cache[pages].reshape(past, ATTN_HEADS, V_DIM)
        kall = jnp.concatenate([kp, ks.astype(kp.dtype)], axis=0)
        vall = jnp.concatenate([vp, vs.astype(vp.dtype)], axis=0)
        s = jnp.einsum('qhd,khd->hqk', qs, kall, preferred_element_type=jnp.float32) * SCALE
        a = _diff_weights(s, mask, lam)
        return jnp.einsum('hqk,khd->qhd', a, vall.astype(jnp.float32)).astype(vs.dtype)

    return one


def _conv_tail(c_ext, w_dw, b_dw, g_ln, b_ln, w_pw2, b_pw2):
    y = lax.conv_general_dilated(c_ext, w_dw[:, None, :].astype(c_ext.dtype), window_strides=(1,),
                                 padding='VALID', dimension_numbers=('NWC', 'WIO', 'NWC'),
                                 feature_group_count=D_CONV) + b_dw
    y = jax.nn.silu(_layernorm(y, g_ln, b_ln))
    return jnp.einsum('btc,ce->bte', y, w_pw2) + b_pw2


def _mixer(x, pos, attend, conv_ctx, g_norm, w_in, g_q, g_k, g_sub, b_glu, w_dw, b_dw,
           g_ln, b_ln, w_pw2, b_pw2, w_out, lam_init):
    B, T, _ = x.shape
    h = _rms(x, g_norm)
    z = jnp.einsum('btd,de->bte', h, w_in)
    q, k, v, gate_a, u, gate_c = jnp.split(z, SPLIT_POINTS, axis=-1)
    q = _rope(_rms(q.reshape(B, T, N_QK_HEADS, QK_DIM), g_q), pos)
    k = _rope(_rms(k.reshape(B, T, N_QK_HEADS, QK_DIM), g_k), pos)
    v = v.reshape(B, T, ATTN_HEADS, V_DIM)
    o = attend(q, k, v)
    o = _rms(o, g_sub, 1e-5).reshape(B, T, D_ATTN) * (1.0 - lam_init)
    u = u + b_glu
    c = u[..., :D_CONV] * jax.nn.sigmoid(u[..., D_CONV:])
    c_ext = jnp.concatenate([conv_ctx.astype(c.dtype), c], axis=1)
    cv = _conv_tail(c_ext, w_dw, b_dw, g_ln, b_ln, w_pw2, b_pw2)
    mix = jnp.concatenate([o * jax.nn.silu(gate_a), cv * jax.nn.silu(gate_c)], axis=-1)
    y = x + jnp.einsum('bte,ed->btd', mix, w_out)
    return y, k, v, c_ext[:, -(CONV_WIDTH - 1):]


def setup_inputs(seed: int = 0) -> dict:
    key = jax.random.key(seed)
    ks = jax.random.split(key, 32)
    n_pages = PAST_LEN // PAGE_SIZE
    n_used = DEC_BATCH * n_pages
    n_pool = n_used + max(1, n_used // 4)
    f32 = jnp.float32
    nrm = lambda k, shape, s: jax.random.normal(k, shape, f32) * s
    perm = jax.random.permutation(ks[0], n_pool)[:n_used]
    page_table = perm.reshape(DEC_BATCH, n_pages).astype(jnp.int32)
    return {
        'x_prompt': nrm(ks[1], (BATCH, SEQ, D_MODEL), 1.0),
        'x_sample': nrm(ks[2], (DEC_BATCH, DEC_SEQ, D_MODEL), 1.0),
        'cache_k': nrm(ks[3], (DEPTH, n_pool, PAGE_SIZE, N_QK_HEADS, QK_DIM), 1.0),
        'cache_v': nrm(ks[4], (DEPTH, n_pool, PAGE_SIZE, ATTN_HEADS, V_DIM), 1.0),
        'state_conv': nrm(ks[5], (DEPTH, DEC_BATCH, CONV_WIDTH - 1, D_CONV), 0.5),
        'page_table': page_table,
        'g_norm': 1.0 + nrm(ks[6], (DEPTH, D_MODEL), 0.01),
        'w_in': nrm(ks[7], (DEPTH, D_MODEL, D_IN), D_MODEL ** -0.5),
        'g_q': 1.0 + nrm(ks[8], (DEPTH, QK_DIM), 0.01),
        'g_k': 1.0 + nrm(ks[9], (DEPTH, QK_DIM), 0.01),
        'lambda_q1': nrm(ks[10], (DEPTH, QK_DIM), 0.1),
        'lambda_k1': nrm(ks[11], (DEPTH, QK_DIM), 0.1),
        'lambda_q2': nrm(ks[12], (DEPTH, QK_DIM), 0.1),
        'lambda_k2': nrm(ks[13], (DEPTH, QK_DIM), 0.1),
        'g_sub': 1.0 + nrm(ks[14], (DEPTH, V_DIM), 0.01),
        'b_glu': nrm(ks[15], (DEPTH, 2 * D_CONV), 0.01),
        'w_dw': nrm(ks[16], (DEPTH, CONV_WIDTH, D_CONV), CONV_WIDTH ** -0.5),
        'b_dw': nrm(ks[17], (DEPTH, D_CONV), 0.01),
        'g_ln': 1.0 + nrm(ks[18], (DEPTH, D_CONV), 0.01),
        'b_ln': nrm(ks[19], (DEPTH, D_CONV), 0.01),
        'w_pw2': nrm(ks[20], (DEPTH, D_CONV, D_CONV), D_CONV ** -0.5),
        'b_pw2': nrm(ks[21], (DEPTH, D_CONV), 0.01),
        'w_out': nrm(ks[22], (DEPTH, D_MIX, D_MODEL), D_MIX ** -0.5),
    }


def reference(x_prompt, x_sample, cache_k, cache_v, state_conv, page_table,
              g_norm, w_in, g_q, g_k, lambda_q1, lambda_k1, lambda_q2, lambda_k2, g_sub,
              b_glu, w_dw, b_dw, g_ln, b_ln, w_pw2, b_pw2, w_out):
    Bp, Tp, _ = x_prompt.shape
    Bs, Ts, _ = x_sample.shape
    past = page_table.shape[1] * PAGE_SIZE
    pos_p = jnp.arange(Tp)
    pos_s = past + jnp.arange(Ts)
    yp, ys = x_prompt, x_sample
    kp_l, vp_l, cp_l, ks_l, vs_l, cs_l = [], [], [], [], [], []
    for l in range(DEPTH):
        lam_init = 0.8 - 0.6 * math.exp(-0.3 * l)
        lam = (jnp.exp(jnp.sum(lambda_q1[l].astype(jnp.float32) * lambda_k1[l].astype(jnp.float32)))
               - jnp.exp(jnp.sum(lambda_q2[l].astype(jnp.float32) * lambda_k2[l].astype(jnp.float32)))
               + lam_init)
        shared = (g_norm[l], w_in[l], g_q[l], g_k[l], g_sub[l], b_glu[l], w_dw[l], b_dw[l],
                  g_ln[l], b_ln[l], w_pw2[l], b_pw2[l], w_out[l], lam_init)
        attend_p = lambda q, k, v: _prompt_attn(q, k, v, lam)
        zero_ctx = jnp.zeros((Bp, CONV_WIDTH - 1, D_CONV), x_prompt.dtype)
        yp, kp, vp, cp = _mixer(yp, pos_p, attend_p, zero_ctx, *shared)
        one = _sample_attn(None, None, None, cache_k[l], cache_v[l], page_table) if False else None
        k_cache_l, v_cache_l = cache_k[l], cache_v[l]

        def attend_s(q, k, v, k_cache_l=k_cache_l, v_cache_l=v_cache_l):
            fn = _sample_attn(q, k, v, k_cache_l, v_cache_l, page_table)
            lams = jnp.broadcast_to(lam, (q.shape[0],))
            return lax.map(fn, (q, k, v, page_table, lams))

        ys, ksn, vsn, csn = _mixer(ys, pos_s, attend_s, state_conv[l], *shared)
        kp_l.append(kp); vp_l.append(vp); cp_l.append(cp)
        ks_l.append(ksn); vs_l.append(vsn); cs_l.append(csn)
    k_prompt = jnp.stack(kp_l)
    v_prompt = jnp.stack(vp_l)
    conv_prompt = jnp.stack(cp_l)
    k_sample = jnp.stack(ks_l)
    v_sample = jnp.stack(vs_l)
    conv_sample = jnp.stack(cs_l)
    return (yp, ys, k_prompt, v_prompt, conv_prompt, k_sample, v_sample, conv_sample)
```

```python
import functools
import math

import jax
import jax.numpy as jnp
from jax import lax
from jax.experimental import pallas as pl
from jax.experimental.pallas import tpu as pltpu

D_MODEL = 2048
D_ATTN = D_MODEL // 2
D_CONV = D_MODEL - D_ATTN
ATTN_HEADS = 8
N_QK_HEADS = 2 * ATTN_HEADS
QK_DIM = D_ATTN // N_QK_HEADS
V_DIM = 2 * QK_DIM
ROT_DIM = QK_DIM // 4
ROT_HALF = ROT_DIM // 2
ROPE_THETA = 500000.0
CONV_WIDTH = 31
CONV_CTX = CONV_WIDTH - 1
PAGE_SIZE = 128
SCALE = QK_DIM ** -0.5

LANES = 128
SUBLANES = 8
BF16_ROWS = 16
MXU_DEPTH = 256
CTX_ROWS = 32
CTX_PAD = CTX_ROWS - CONV_CTX
VMEM_INTERNAL_BYTES = 12 << 20

PROJ_ROWS = 256
ATTN_ROWS = 256
TAIL_ROWS = 256
CONV_CHUNK = 32
DEC_PAGES = 8
DEC_SEQS = 8

HEADS_PER_PASS = MXU_DEPTH // QK_DIM
N_PASSES = N_QK_HEADS // HEADS_PER_PASS

F32 = jnp.float32
BF16 = jnp.bfloat16
NT_DIMS = (((1,), (1,)), ((), ()))


def _vmem_limit(pipelined_bytes, resident_bytes):
    return int(2 * pipelined_bytes + resident_bytes + VMEM_INTERNAL_BYTES)


def _nbytes(shape, dtype):
    return math.prod(shape) * jnp.dtype(dtype).itemsize


def _sigmoid(x):
    return 1.0 / (1.0 + jnp.exp(-x))


def _silu(x):
    return x * _sigmoid(x)


def _resident(shape, index_map):
    return pl.BlockSpec(shape, index_map, pipeline_mode=pl.Buffered(1))


def _proj_kernel(x_ref, gn_ref, w_ref, wkt_ref, bd_ref, gq_ref, gkt_ref, cos_ref, s1_ref, s2_ref,
                 cost_ref, sint_ref, bglu_ref,
                 q_ref, kt_ref, kbt_ref, v_ref, vb_ref, ga_ref, c_ref, gc_ref, h_scr):
    x = x_ref[...]
    ms = jnp.mean(x * x, axis=-1, keepdims=True)
    h_scr[...] = (x * lax.rsqrt(ms + 1e-6) * gn_ref[...]).astype(BF16)

    def group(g):
        return jnp.dot(h_scr[...], w_ref[:, g * D_ATTN:(g + 1) * D_ATTN],
                       preferred_element_type=F32)

    zq = group(0)
    for c in range(D_ATTN // LANES):
        zc = zq[:, c * LANES:(c + 1) * LANES]
        ss = jnp.dot((zc * zc).astype(BF16), bd_ref[...], preferred_element_type=F32)
        y = zc * lax.rsqrt(ss * (1.0 / QK_DIM) + 1e-6) * gq_ref[...]
        y = (y * cos_ref[...]
             + pltpu.roll(y, LANES - ROT_HALF, 1) * s1_ref[...]
             + pltpu.roll(y, ROT_HALF, 1) * s2_ref[...])
        q_ref[:, c * LANES:(c + 1) * LANES] = (y * SCALE).astype(BF16)

    zkt = lax.dot_general(wkt_ref[...], h_scr[...], NT_DIMS, preferred_element_type=F32)
    for hd in range(N_QK_HEADS):
        blk = zkt[hd * QK_DIM:(hd + 1) * QK_DIM, :]
        ss = jnp.sum(blk * blk, axis=0, keepdims=True)
        y = blk * lax.rsqrt(ss * (1.0 / QK_DIM) + 1e-6) * gkt_ref[...]
        x1, x2 = y[0:ROT_HALF], y[ROT_HALF:ROT_DIM]
        cos, sin = cost_ref[...], sint_ref[...]
        y = jnp.concatenate([x1 * cos - x2 * sin, x2 * cos + x1 * sin, y[ROT_DIM:]], axis=0)
        kt_ref[0, hd * QK_DIM:(hd + 1) * QK_DIM, :] = y
        kbt_ref[0, hd * QK_DIM:(hd + 1) * QK_DIM, :] = y.astype(BF16)

    zv = group(1)
    v_ref[...] = zv
    vb_ref[...] = zv.astype(BF16)
    ga_ref[...] = _silu(group(2)).astype(BF16)
    u = group(3) + bglu_ref[:, :D_CONV]
    c_ref[...] = u * _sigmoid(group(4) + bglu_ref[:, D_CONV:])
    gc_ref[...] = _silu(group(5)).astype(BF16)


def _rope_tables(pos):
    inv = jnp.power(ROPE_THETA, -jnp.arange(ROT_HALF, dtype=F32) * (2.0 / ROT_DIM))
    ang = pos.astype(F32)[:, None] * inv[None, :]
    cos, sin = jnp.cos(ang), jnp.sin(ang)
    n = pos.shape[0]
    rest = QK_DIM - ROT_DIM
    zeros = jnp.zeros((n, ROT_HALF), F32)
    cos_t = jnp.concatenate([cos, cos, jnp.ones((n, rest), F32)], axis=1)
    s1_t = jnp.concatenate([-sin, zeros, jnp.zeros((n, rest), F32)], axis=1)
    s2_t = jnp.concatenate([zeros, sin, jnp.zeros((n, rest), F32)], axis=1)
    rep = LANES // QK_DIM
    rows = tuple(jnp.tile(t, (1, rep)) for t in (cos_t, s1_t, s2_t))
    return rows + (cos.T, sin.T)


def _proj(x2d, pos, g_norm, w_groups_bf, wkt_bf, g_q, g_k, b_glu):
    m = x2d.shape[0]
    p = pos.shape[0]
    tm = min(PROJ_ROWS, m)
    assert m % p == 0 and p % tm == 0
    n_pos_blocks = p // tm
    cos_t, s1_t, s2_t, cos_tt, sin_tt = _rope_tables(pos)
    head = jnp.arange(LANES) // QK_DIM
    bd = (head[:, None] == head[None, :]).astype(BF16)
    gq = jnp.tile(g_q.astype(F32), LANES // QK_DIM)[None, :]
    gkt = jnp.broadcast_to(g_k.astype(F32)[:, None], (QK_DIM, tm))
    n_w = w_groups_bf.shape[1]

    row = lambda i: (i, 0)
    const = lambda i: (0, 0)
    tab = lambda i: (i % n_pos_blocks, 0)
    tab_t = lambda i: (0, i % n_pos_blocks)
    kt_map = lambda i: (i // n_pos_blocks, 0, i % n_pos_blocks)
    in_specs = [
        pl.BlockSpec((tm, D_MODEL), row),
        _resident((1, D_MODEL), const),
        _resident((D_MODEL, n_w), const),
        _resident((D_ATTN, D_MODEL), const),
        _resident((LANES, LANES), const),
        _resident((1, LANES), const),
        _resident((QK_DIM, tm), const),
        pl.BlockSpec((tm, LANES), tab),
        pl.BlockSpec((tm, LANES), tab),
        pl.BlockSpec((tm, LANES), tab),
        pl.BlockSpec((ROT_HALF, tm), tab_t),
        pl.BlockSpec((ROT_HALF, tm), tab_t),
        _resident((1, 2 * D_CONV), const),
    ]
    rows_out = lambda dt: (jax.ShapeDtypeStruct((m, D_ATTN), dt), pl.BlockSpec((tm, D_ATTN), row))
    kt_out = lambda dt: (jax.ShapeDtypeStruct((m // p, D_ATTN, p), dt),
                         pl.BlockSpec((1, D_ATTN, tm), kt_map))
    outs = [rows_out(BF16), kt_out(F32), kt_out(BF16), rows_out(F32), rows_out(BF16),
            rows_out(BF16), rows_out(F32), rows_out(BF16)]
    pipelined = (_nbytes((tm, D_MODEL), F32) + 3 * _nbytes((tm, LANES), F32)
                 + sum(_nbytes((tm, D_ATTN), o[0].dtype) for o in outs))
    resident = (_nbytes((D_MODEL, n_w), BF16) + _nbytes((D_ATTN, D_MODEL), BF16)
                + _nbytes((tm, D_MODEL), BF16) + 4 * _nbytes((tm, D_ATTN), F32))
    return pl.pallas_call(
        _proj_kernel,
        grid=(m // tm,),
        in_specs=in_specs,
        out_specs=tuple(o[1] for o in outs),
        out_shape=tuple(o[0] for o in outs),
        scratch_shapes=[pltpu.VMEM((tm, D_MODEL), BF16)],
        compiler_params=pltpu.CompilerParams(
            dimension_semantics=("arbitrary",),
            vmem_limit_bytes=_vmem_limit(pipelined, resident)),
        name="proj",
    )(x2d, g_norm.astype(F32)[None, :], w_groups_bf, wkt_bf, bd, gq, gkt, cos_t, s1_t, s2_t,
      cos_tt, sin_tt, b_glu.astype(F32)[None, :])


def _lambda_full(lq1_ref, lk1_ref, lq2_ref, lk2_ref, lam_init):
    a = jnp.sum(lq1_ref[...] * lk1_ref[...], axis=-1, keepdims=True)
    b = jnp.sum(lq2_ref[...] * lk2_ref[...], axis=-1, keepdims=True)
    return jnp.exp(a) - jnp.exp(b) + lam_init


def _subnorm_gate(o, gsub, gate, lam_init):
    ms = jnp.mean(o * o, axis=-1, keepdims=True)
    y = o * lax.rsqrt(ms + 1e-5) * gsub
    return y * (1.0 - lam_init) * gate


def _prompt_attn_kernel(lq1_ref, lk1_ref, lq2_ref, lk2_ref, gsub_ref, q_ref, kt_ref, v_ref, ga_ref,
                        o_ref, m_scr, l_scr, acc_scr, *, tq, lam_init):
    qi = pl.program_id(2)
    q = q_ref[0]
    lane = lax.broadcasted_iota(jnp.int32, (tq, LANES), 1)
    zero = jnp.zeros_like(q)
    q2 = jnp.concatenate([jnp.where(lane < QK_DIM, q, zero),
                          jnp.where(lane >= QK_DIM, q, zero)], axis=0)
    m_scr[...] = jnp.full(m_scr.shape, -jnp.inf, F32)
    l_scr[...] = jnp.zeros(l_scr.shape, F32)
    acc_scr[...] = jnp.zeros(acc_scr.shape, F32)

    def step(j, masked):
        start = pl.multiple_of(j * tq, tq)
        kt = kt_ref[0, :, pl.ds(start, tq)]
        vt = v_ref[0, pl.ds(start, tq), :]
        s = jnp.dot(q2, kt, preferred_element_type=F32)
        if masked:
            row = lax.broadcasted_iota(jnp.int32, s.shape, 0)
            col = lax.broadcasted_iota(jnp.int32, s.shape, 1)
            row = jnp.where(row >= tq, row - tq, row)
            s = jnp.where(col <= row, s, -jnp.inf)
        m_old = m_scr[...]
        m_new = jnp.maximum(m_old, jnp.max(s, axis=1, keepdims=True))
        alpha = jnp.exp(m_old - m_new)
        p = jnp.exp(s - m_new)
        l_scr[...] = alpha * l_scr[...] + jnp.sum(p, axis=1, keepdims=True)
        acc_scr[...] = alpha * acc_scr[...] + jnp.dot(p.astype(BF16), vt,
                                                      preferred_element_type=F32)
        m_scr[...] = m_new

    def body(j, carry):
        step(j, False)
        return carry

    lax.fori_loop(0, qi, body, 0)
    step(qi, True)

    o_all = acc_scr[...] / l_scr[...]
    lam = _lambda_full(lq1_ref, lk1_ref, lq2_ref, lk2_ref, lam_init)
    o = o_all[:tq] - lam * o_all[tq:]
    y = _subnorm_gate(o, gsub_ref[...], ga_ref[0].astype(F32), lam_init)
    o_ref[0] = y.astype(BF16)


def _prompt_attn(q, kbt, vb, ga, lams, g_sub, lam_init):
    b, t, _ = q.shape
    tq = min(ATTN_ROWS, t)
    assert t % tq == 0
    small = pl.BlockSpec((1, QK_DIM), lambda bi, h, qi: (0, 0))
    tile = pl.BlockSpec((1, tq, LANES), lambda bi, h, qi: (bi, qi, h))
    keys = pl.BlockSpec((1, LANES, t), lambda bi, h, qi: (bi, h, 0))
    vals = pl.BlockSpec((1, t, LANES), lambda bi, h, qi: (bi, 0, h))
    pipelined = 3 * _nbytes((tq, LANES), BF16) + 2 * _nbytes((t, LANES), BF16)
    resident = 3 * _nbytes((2 * tq, LANES), F32) + 4 * _nbytes((2 * tq, tq), F32)
    return pl.pallas_call(
        functools.partial(_prompt_attn_kernel, tq=tq, lam_init=lam_init),
        grid=(b, ATTN_HEADS, t // tq),
        in_specs=[small, small, small, small,
                  pl.BlockSpec((1, V_DIM), lambda bi, h, qi: (0, 0)),
                  tile, keys, vals, tile],
        out_specs=tile,
        out_shape=jax.ShapeDtypeStruct((b, t, D_ATTN), BF16),
        scratch_shapes=[pltpu.VMEM((2 * tq, 1), F32), pltpu.VMEM((2 * tq, 1), F32),
                        pltpu.VMEM((2 * tq, V_DIM), F32)],
        compiler_params=pltpu.CompilerParams(
            dimension_semantics=("arbitrary", "arbitrary", "arbitrary"),
            vmem_limit_bytes=_vmem_limit(pipelined, resident)),
        name="prompt_attn",
    )(*lams, g_sub.astype(F32)[None, :], q, kbt, vb, ga)


def _decode_attn_kernel(pt_ref, lq1_ref, lk1_ref, lq2_ref, lk2_ref, gsub_ref, qc_ref, ktn_ref, vn_ref,
                        ga_ref, *refs, n_pages_step, n_q, lam_init):
    k_refs = refs[:n_pages_step]
    v_refs = refs[n_pages_step:2 * n_pages_step]
    o_ref, m_scr, l_scr, acc_scr = refs[2 * n_pages_step:]
    del pt_ref
    seq = pl.program_id(0)
    s_idx = pl.program_id(1)
    rows_pass = HEADS_PER_PASS * n_q
    rows_h = 2 * n_q

    @pl.when(s_idx == 0)
    def _():
        m_scr[...] = jnp.full(m_scr.shape, -jnp.inf, F32)
        l_scr[...] = jnp.zeros(l_scr.shape, F32)
        acc_scr[...] = jnp.zeros(acc_scr.shape, F32)

    def scores(kt):
        return jnp.concatenate(
            [jnp.dot(qc_ref[0, c * rows_pass:(c + 1) * rows_pass, :],
                     kt[c * MXU_DEPTH:(c + 1) * MXU_DEPTH, :], preferred_element_type=F32)
             for c in range(N_PASSES)], axis=0)

    def update(s, values_of_head, l_lanes):
        m_old = m_scr[...]
        m_new = jnp.maximum(m_old, jnp.max(s, axis=1, keepdims=True))
        alpha = jnp.exp(m_old - m_new)
        p = jnp.exp(s - m_new)
        p_sum = p[:, 0:l_lanes]
        for t in range(1, s.shape[1] // l_lanes):
            p_sum = p_sum + p[:, t * l_lanes:(t + 1) * l_lanes]
        if l_lanes < LANES:
            p_sum = jnp.concatenate([p_sum, jnp.zeros((s.shape[0], LANES - l_lanes), F32)], axis=1)
        l_scr[...] = alpha * l_scr[...] + p_sum
        pb = p.astype(BF16)
        pv = jnp.concatenate(
            [jnp.dot(pb[h * rows_h:(h + 1) * rows_h, :], values_of_head(h),
                     preferred_element_type=F32) for h in range(ATTN_HEADS)], axis=0)
        acc_scr[...] = alpha * acc_scr[...] + pv
        m_scr[...] = m_new

    s_pages = jnp.concatenate([scores(k_refs[g][0].astype(BF16)) for g in range(n_pages_step)],
                              axis=1)

    def cached_values(h):
        return jnp.concatenate(
            [v_refs[g][0, pl.ds(h, PAGE_SIZE, stride=ATTN_HEADS), :].astype(BF16)
             for g in range(n_pages_step)], axis=0)

    update(s_pages, cached_values, LANES)

    @pl.when(s_idx == pl.num_programs(1) - 1)
    def _():
        s = scores(ktn_ref[...])
        n_new = s.shape[1]
        tok = lax.broadcasted_iota(jnp.int32, s.shape, 1)
        qry = lax.broadcasted_iota(jnp.int32, s.shape, 0) % n_q
        lo = seq * n_q
        visible = (tok >= lo) & (tok <= lo + qry)
        update(jnp.where(visible, s, -jnp.inf),
               lambda h: vn_ref[:, h * V_DIM:(h + 1) * V_DIM], min(n_new, LANES))

        lam = _lambda_full(lq1_ref, lk1_ref, lq2_ref, lk2_ref, lam_init)
        o_all = acc_scr[...] / jnp.sum(l_scr[...], axis=1, keepdims=True)
        for h in range(ATTN_HEADS):
            o = (o_all[h * rows_h:h * rows_h + n_q]
                 - lam * o_all[h * rows_h + n_q:(h + 1) * rows_h])
            gate = ga_ref[:, h * V_DIM:(h + 1) * V_DIM]
            o_ref[:, h * V_DIM:(h + 1) * V_DIM] = _subnorm_gate(o, gsub_ref[...], gate, lam_init)


def _decode_attn(q, ktn, vn, ga, cache_kt, cache_v, page_table, lams, g_sub, lam_init):
    bd, s, _ = q.shape
    n_pages = page_table.shape[1]
    g = min(DEC_PAGES, n_pages)
    assert n_pages % g == 0 and s == SUBLANES
    n_rows = N_QK_HEADS * s
    q6 = q.reshape(bd, s, N_PASSES, HEADS_PER_PASS, QK_DIM)
    eye = jnp.eye(HEADS_PER_PASS, dtype=q.dtype)
    qc = jnp.einsum('bqchd,hg->bchqgd', q6, eye).reshape(bd, n_rows, MXU_DEPTH)

    small = pl.BlockSpec((1, QK_DIM), lambda b, i, pt: (0, 0))
    const = lambda shape: pl.BlockSpec(shape, lambda b, i, pt: (0, 0))
    per_seq = pl.BlockSpec((s, D_ATTN), lambda b, i, pt: (b, 0))

    def page_spec(gi):
        return pl.BlockSpec((1, D_ATTN, PAGE_SIZE), lambda b, i, pt: (pt[b, i * g + gi], 0, 0))

    in_specs = ([small, small, small, small, const((1, V_DIM)),
                 pl.BlockSpec((1, n_rows, MXU_DEPTH), lambda b, i, pt: (b, 0, 0)),
                 const((D_ATTN, bd * s)), const((bd * s, D_ATTN)), per_seq]
                + [page_spec(gi) for gi in range(g)] + [page_spec(gi) for gi in range(g)])
    pipelined = (2 * g * _nbytes((PAGE_SIZE, D_ATTN), F32) + _nbytes((n_rows, MXU_DEPTH), BF16)
                 + 2 * _nbytes((bd * s, D_ATTN), BF16) + 2 * _nbytes((s, D_ATTN), F32))
    resident = (2 * _nbytes((g * PAGE_SIZE, D_ATTN), BF16) + 3 * _nbytes((n_rows, LANES), F32)
                + 4 * _nbytes((n_rows, g * PAGE_SIZE), F32))
    grid_spec = pltpu.PrefetchScalarGridSpec(
        num_scalar_prefetch=1,
        grid=(bd, n_pages // g),
        in_specs=in_specs,
        out_specs=pl.BlockSpec((s, D_ATTN), lambda b, i, pt: (b, 0)),
        scratch_shapes=[pltpu.VMEM((n_rows, 1), F32), pltpu.VMEM((n_rows, LANES), F32),
                        pltpu.VMEM((n_rows, V_DIM), F32)],
    )
    return pl.pallas_call(
        functools.partial(_decode_attn_kernel, n_pages_step=g, n_q=s, lam_init=lam_init),
        grid_spec=grid_spec,
        out_shape=jax.ShapeDtypeStruct((bd * s, D_ATTN), F32),
        compiler_params=pltpu.CompilerParams(
            dimension_semantics=("arbitrary", "arbitrary"),
            vmem_limit_bytes=_vmem_limit(pipelined, resident)),
        name="decode_attn",
    )(page_table, *lams, g_sub.astype(F32)[None, :], qc, ktn, vn, ga,
      *([cache_kt] * g), *([cache_v] * g))


def _tail_kernel(ctx_ref, c_ref, gc_ref, ma_ref, x_ref, wdw_ref, bdw_ref, gln_ref, bln_ref,
                 wpw_ref, bpw_ref, wout_ref, y_ref, ext_scr, sh_scr, cv_scr, *, bs, tt, rc, zero_first):
    i = pl.program_id(1)
    sh_rows = tt + CTX_ROWS - SUBLANES
    for b in range(bs):
        ctx = ctx_ref[b]
        if zero_first:
            ctx = jnp.where(i == 0, jnp.zeros_like(ctx), ctx)
        ext_scr[0:CTX_ROWS, :] = ctx
        ext_scr[CTX_ROWS:CTX_ROWS + tt, :] = c_ref[b * tt:(b + 1) * tt, :]
        for s in range(1, SUBLANES):
            sh_scr[s - 1, 0:sh_rows, :] = ext_scr[s:s + sh_rows, :]

        def chunk(ci, carry, b=b):
            r0 = pl.multiple_of(ci * rc, rc)
            acc = jnp.broadcast_to(bdw_ref[...], (rc, D_CONV))
            for tap in range(CONV_WIDTH):
                whole, s = divmod(CTX_PAD + tap, SUBLANES)
                rows = pl.ds(r0 + whole * SUBLANES, rc)
                win = ext_scr[rows, :] if s == 0 else sh_scr[s - 1, rows, :]
                acc = acc + win * wdw_ref[tap:tap + 1, :]
            cv_scr[pl.ds(b * tt + r0, rc), :] = acc
            return carry

        lax.fori_loop(0, tt // rc, chunk, 0)

    cv = cv_scr[...]
    mu = jnp.mean(cv, axis=-1, keepdims=True)
    d = cv - mu
    var = jnp.mean(d * d, axis=-1, keepdims=True)
    yn = d * lax.rsqrt(var + 1e-5) * gln_ref[...] + bln_ref[...]
    act = _silu(yn).astype(BF16)
    pw = jnp.dot(act, wpw_ref[...], preferred_element_type=F32) + bpw_ref[...]
    mix_c = (pw * gc_ref[...].astype(F32)).astype(BF16)
    mix_a = ma_ref[...].astype(BF16)
    y_ref[...] = (x_ref[...]
                  + jnp.dot(mix_a, wout_ref[0:D_ATTN, :], preferred_element_type=F32)
                  + jnp.dot(mix_c, wout_ref[D_ATTN:, :], preferred_element_type=F32))


def _tail(ctx, ctx_map, c, gc, mix_a, x, w_dw, b_dw, g_ln, b_ln, w_pw_bf, b_pw2, w_out_bf,
          *, b, t, bs, tt, zero_first):
    assert b % bs == 0 and t % tt == 0 and (bs == 1 or tt == t)
    rc = min(CONV_CHUNK, tt)
    assert tt % rc == 0
    tiles_per_seq = t // tt
    tile = lambda width: pl.BlockSpec((bs * tt, width), lambda bi, i: (bi * tiles_per_seq + i, 0))
    const = lambda shape: _resident(shape, lambda bi, i: (0, 0))
    in_specs = [pl.BlockSpec((bs, CTX_ROWS, D_CONV), ctx_map),
                tile(D_CONV), tile(D_CONV), tile(D_ATTN), tile(D_MODEL),
                const((CONV_WIDTH, D_CONV)), const((1, D_CONV)), const((1, D_CONV)),
                const((1, D_CONV)), const((D_CONV, D_CONV)), const((1, D_CONV)),
                const((D_MODEL, D_MODEL))]
    rows = bs * tt
    ext_shape = (CTX_ROWS + tt, D_CONV)
    sh_shape = (SUBLANES - 1,) + ext_shape
    pipelined = (_nbytes((bs, CTX_ROWS, D_CONV), F32) + _nbytes((rows, D_CONV), F32)
                 + _nbytes((rows, D_CONV), gc.dtype) + _nbytes((rows, D_ATTN), mix_a.dtype)
                 + 2 * _nbytes((rows, D_MODEL), F32))
    resident = (_nbytes((CTX_ROWS, D_CONV), F32) + _nbytes((D_CONV, D_CONV), BF16)
                + _nbytes((D_MODEL, D_MODEL), BF16) + _nbytes(ext_shape, F32)
                + _nbytes(sh_shape, F32) + 4 * _nbytes((rows, D_CONV), F32))
    return pl.pallas_call(
        functools.partial(_tail_kernel, bs=bs, tt=tt, rc=rc, zero_first=zero_first),
        grid=(b // bs, tiles_per_seq),
        in_specs=in_specs,
        out_specs=tile(D_MODEL),
        out_shape=jax.ShapeDtypeStruct((b * t, D_MODEL), F32),
        scratch_shapes=[pltpu.VMEM(ext_shape, F32), pltpu.VMEM(sh_shape, F32),
                        pltpu.VMEM((rows, D_CONV), F32)],
        compiler_params=pltpu.CompilerParams(
            dimension_semantics=("arbitrary", "arbitrary"),
            vmem_limit_bytes=_vmem_limit(pipelined, resident)),
        name="tail",
    )(ctx, c, gc, mix_a, x, w_dw.astype(F32), b_dw.astype(F32)[None, :], g_ln.astype(F32)[None, :],
      b_ln.astype(F32)[None, :], w_pw_bf, b_pw2.astype(F32)[None, :], w_out_bf)


def kernel(x_prompt, x_sample, cache_k, cache_v, state_conv, page_table, g_norm, w_in, g_q, g_k, lambda_q1, lambda_k1, lambda_q2, lambda_k2, g_sub, b_glu, w_dw, b_dw, g_ln, b_ln, w_pw2, b_pw2, w_out):
    bp, tp, _ = x_prompt.shape
    bs, ts, _ = x_sample.shape
    depth = w_in.shape[0]
    n_pool = cache_k.shape[1]
    past = page_table.shape[1] * PAGE_SIZE
    pos_p = jnp.arange(tp)
    pos_s = jnp.tile(past + jnp.arange(ts), bs)
    tt_p = min(TAIL_ROWS, tp)
    seqs = min(DEC_SEQS, bs)
    mp, ms = bp * tp, bs * ts

    yp, ys = x_prompt.reshape(mp, D_MODEL), x_sample.reshape(ms, D_MODEL)
    outs = [[] for _ in range(6)]
    for l in range(depth):
        lam_init = 0.8 - 0.6 * math.exp(-0.3 * l)
        w_l = w_in[l].astype(BF16)
        w_groups = jnp.concatenate([w_l[:, :D_ATTN], w_l[:, 2 * D_ATTN:]], axis=1)
        wkt = w_l[:, D_ATTN:2 * D_ATTN].T
        w_pw_bf = w_pw2[l].astype(BF16)
        w_out_bf = w_out[l].astype(BF16)
        lams = tuple(v[l].astype(F32)[None, :] for v in (lambda_q1, lambda_k1, lambda_q2, lambda_k2))
        tail_w = (w_dw[l], b_dw[l], g_ln[l], b_ln[l], w_pw_bf, b_pw2[l], w_out_bf)
        proj_w = (g_norm[l], w_groups, wkt, g_q[l], g_k[l], b_glu[l])

        q, kt, kbt, v, vb, ga, c, gc = _proj(yp, pos_p, *proj_w)
        as3 = lambda a: a.reshape(bp, tp, D_ATTN)
        mix_a = _prompt_attn(as3(q), kbt, as3(vb), as3(ga), lams, g_sub[l], lam_init)
        c3 = as3(c)
        blocks_per_tile = tt_p // CTX_ROWS
        prev_block = lambda bi, i: (bi, jnp.maximum(i * blocks_per_tile - 1, 0), 0)
        yp = _tail(c3, prev_block, c, gc, mix_a.reshape(mp, D_ATTN), yp, *tail_w,
                   b=bp, t=tp, bs=1, tt=tt_p, zero_first=True)
        k_rows = kt.reshape(bp, N_QK_HEADS, QK_DIM, tp).transpose(0, 3, 1, 2)
        outs[0].append(k_rows)
        outs[1].append(v.reshape(bp, tp, ATTN_HEADS, V_DIM))
        outs[2].append(c3[:, tp - CONV_CTX:])

        q, kt, kbt, v, vb, ga, c, gc = _proj(ys, pos_s, *proj_w)
        cache_kt = cache_k[l].transpose(0, 2, 3, 1).reshape(n_pool, D_ATTN, PAGE_SIZE)
        cache_vr = cache_v[l].reshape(n_pool, PAGE_SIZE * ATTN_HEADS, V_DIM)
        mix_a = _decode_attn(q.reshape(bs, ts, D_ATTN), kbt[0], vb, ga.astype(F32),
                             cache_kt, cache_vr, page_table, lams, g_sub[l], lam_init)
        ctx = jnp.pad(state_conv[l].astype(F32), ((0, 0), (CTX_PAD, 0), (0, 0)))
        ys = _tail(ctx, lambda bi, i: (bi, 0, 0), c, gc, mix_a, ys, *tail_w,
                   b=bs, t=ts, bs=seqs, tt=ts, zero_first=False)
        k_rows = kt[0].T.reshape(bs, ts, N_QK_HEADS, QK_DIM)
        outs[3].append(k_rows)
        outs[4].append(v.reshape(bs, ts, ATTN_HEADS, V_DIM))
        c3 = c.reshape(bs, ts, D_CONV)
        outs[5].append(jnp.concatenate([state_conv[l].astype(F32), c3], axis=1)[:, -CONV_CTX:])

    return ((yp.reshape(bp, tp, D_MODEL), ys.reshape(bs, ts, D_MODEL))
            + tuple(jnp.stack(o) for o in outs))
```

```python
import functools
import math

import jax
import jax.numpy as jnp
from jax import lax
from jax.experimental import pallas as pl
from jax.experimental.pallas import tpu as pltpu

D_MODEL = 2048
D_ATTN = D_MODEL // 2
D_CONV = D_MODEL - D_ATTN
ATTN_HEADS = 8
N_QK_HEADS = 2 * ATTN_HEADS
QK_DIM = D_ATTN // N_QK_HEADS
V_DIM = 2 * QK_DIM
ROT_DIM = QK_DIM // 4
ROT_HALF = ROT_DIM // 2
ROPE_THETA = 500000.0
CONV_WIDTH = 31
CONV_CTX = CONV_WIDTH - 1
PAGE_SIZE = 128
SCALE = QK_DIM ** -0.5
Q_SCALE = SCALE * math.log2(math.e)

LANES = 128
SUBLANES = 8
BF16_ROWS = 16
MXU_DEPTH = 256
CTX_ROWS = 32
CTX_PAD = CTX_ROWS - CONV_CTX
VMEM_INTERNAL_BYTES = 12 << 20

PROJ_ROWS = 256
ATTN_ROWS = 512
ATTN_KEYS = 512
ATTN_HEADS_PER_STEP = 2
TAIL_ROWS = 256
CONV_CHUNK = 32
DEC_PAGES = 16
DEC_SEQS = 8

HEADS_PER_PASS = MXU_DEPTH // QK_DIM
N_PASSES = N_QK_HEADS // HEADS_PER_PASS

F32 = jnp.float32
BF16 = jnp.bfloat16
NT_DIMS = (((1,), (1,)), ((), ()))


def _vmem_limit(pipelined_bytes, resident_bytes):
    return int(2 * pipelined_bytes + resident_bytes + VMEM_INTERNAL_BYTES)


def _nbytes(shape, dtype):
    return math.prod(shape) * jnp.dtype(dtype).itemsize


def _sigmoid(x):
    return 1.0 / (1.0 + jnp.exp(-x))


def _silu(x):
    return x * _sigmoid(x)


def _resident(shape, index_map):
    return pl.BlockSpec(shape, index_map, pipeline_mode=pl.Buffered(1))


def _proj_kernel(x_ref, gn_ref, w_ref, wkt_ref, bd_ref, gq_ref, gkt_ref, cos_ref, s1_ref, s2_ref,
                 cost_ref, sint_ref, bglu_ref,
                 q_ref, kt_ref, kbt_ref, v_ref, vb_ref, ga_ref, c_ref, gc_ref, h_scr):
    x = x_ref[...]
    ms = jnp.mean(x * x, axis=-1, keepdims=True)
    h_scr[...] = (x * lax.rsqrt(ms + 1e-6) * gn_ref[...]).astype(BF16)

    def group(g):
        return jnp.dot(h_scr[...], w_ref[:, g * D_ATTN:(g + 1) * D_ATTN],
                       preferred_element_type=F32)

    zq = group(0)
    for c in range(D_ATTN // LANES):
        zc = zq[:, c * LANES:(c + 1) * LANES]
        ss = jnp.dot((zc * zc).astype(BF16), bd_ref[...], preferred_element_type=F32)
        y = zc * lax.rsqrt(ss * (1.0 / QK_DIM) + 1e-6) * gq_ref[...]
        y = (y * cos_ref[...]
             + pltpu.roll(y, LANES - ROT_HALF, 1) * s1_ref[...]
             + pltpu.roll(y, ROT_HALF, 1) * s2_ref[...])
        q_ref[:, c * LANES:(c + 1) * LANES] = (y * Q_SCALE).astype(BF16)

    zkt = lax.dot_general(wkt_ref[...], h_scr[...], NT_DIMS, preferred_element_type=F32)
    for hd in range(N_QK_HEADS):
        blk = zkt[hd * QK_DIM:(hd + 1) * QK_DIM, :]
        ss = jnp.sum(blk * blk, axis=0, keepdims=True)
        y = blk * lax.rsqrt(ss * (1.0 / QK_DIM) + 1e-6) * gkt_ref[...]
        x1, x2 = y[0:ROT_HALF], y[ROT_HALF:ROT_DIM]
        cos, sin = cost_ref[...], sint_ref[...]
        y = jnp.concatenate([x1 * cos - x2 * sin, x2 * cos + x1 * sin, y[ROT_DIM:]], axis=0)
        kt_ref[0, hd * QK_DIM:(hd + 1) * QK_DIM, :] = y
        kbt_ref[0, hd * QK_DIM:(hd + 1) * QK_DIM, :] = y.astype(BF16)

    zv = group(1)
    v_ref[...] = zv
    vb_ref[...] = zv.astype(BF16)
    ga_ref[...] = _silu(group(2)).astype(BF16)
    u = group(3) + bglu_ref[:, :D_CONV]
    c_ref[...] = u * _sigmoid(group(4) + bglu_ref[:, D_CONV:])
    gc_ref[...] = _silu(group(5)).astype(BF16)


def _rope_tables(pos):
    inv = jnp.power(ROPE_THETA, -jnp.arange(ROT_HALF, dtype=F32) * (2.0 / ROT_DIM))
    ang = pos.astype(F32)[:, None] * inv[None, :]
    cos, sin = jnp.cos(ang), jnp.sin(ang)
    n = pos.shape[0]
    rest = QK_DIM - ROT_DIM
    zeros = jnp.zeros((n, ROT_HALF), F32)
    cos_t = jnp.concatenate([cos, cos, jnp.ones((n, rest), F32)], axis=1)
    s1_t = jnp.concatenate([-sin, zeros, jnp.zeros((n, rest), F32)], axis=1)
    s2_t = jnp.concatenate([zeros, sin, jnp.zeros((n, rest), F32)], axis=1)
    rep = LANES // QK_DIM
    rows = tuple(jnp.tile(t, (1, rep)) for t in (cos_t, s1_t, s2_t))
    return rows + (cos.T, sin.T)


def _proj(x2d, pos, g_norm, w_groups_bf, wkt_bf, g_q, g_k, b_glu):
    m = x2d.shape[0]
    p = pos.shape[0]
    tm = min(PROJ_ROWS, m)
    assert m % p == 0 and p % tm == 0
    n_pos_blocks = p // tm
    cos_t, s1_t, s2_t, cos_tt, sin_tt = _rope_tables(pos)
    head = jnp.arange(LANES) // QK_DIM
    bd = (head[:, None] == head[None, :]).astype(BF16)
    gq = jnp.tile(g_q.astype(F32), LANES // QK_DIM)[None, :]
    gkt = jnp.broadcast_to(g_k.astype(F32)[:, None], (QK_DIM, tm))
    n_w = w_groups_bf.shape[1]

    row = lambda i: (i, 0)
    const = lambda i: (0, 0)
    tab = lambda i: (i % n_pos_blocks, 0)
    tab_t = lambda i: (0, i % n_pos_blocks)
    kt_map = lambda i: (i // n_pos_blocks, 0, i % n_pos_blocks)
    in_specs = [
        pl.BlockSpec((tm, D_MODEL), row),
        _resident((1, D_MODEL), const),
        _resident((D_MODEL, n_w), const),
        _resident((D_ATTN, D_MODEL), const),
        _resident((LANES, LANES), const),
        _resident((1, LANES), const),
        _resident((QK_DIM, tm), const),
        pl.BlockSpec((tm, LANES), tab),
        pl.BlockSpec((tm, LANES), tab),
        pl.BlockSpec((tm, LANES), tab),
        pl.BlockSpec((ROT_HALF, tm), tab_t),
        pl.BlockSpec((ROT_HALF, tm), tab_t),
        _resident((1, 2 * D_CONV), const),
    ]
    rows_out = lambda dt: (jax.ShapeDtypeStruct((m, D_ATTN), dt), pl.BlockSpec((tm, D_ATTN), row))
    kt_out = lambda dt: (jax.ShapeDtypeStruct((m // p, D_ATTN, p), dt),
                         pl.BlockSpec((1, D_ATTN, tm), kt_map))
    outs = [rows_out(BF16), kt_out(F32), kt_out(BF16), rows_out(F32), rows_out(BF16),
            rows_out(BF16), rows_out(F32), rows_out(BF16)]
    pipelined = (_nbytes((tm, D_MODEL), F32) + 3 * _nbytes((tm, LANES), F32)
                 + sum(_nbytes((tm, D_ATTN), o[0].dtype) for o in outs))
    resident = (_nbytes((D_MODEL, n_w), BF16) + _nbytes((D_ATTN, D_MODEL), BF16)
                + _nbytes((tm, D_MODEL), BF16) + 4 * _nbytes((tm, D_ATTN), F32))
    return pl.pallas_call(
        _proj_kernel,
        grid=(m // tm,),
        in_specs=in_specs,
        out_specs=tuple(o[1] for o in outs),
        out_shape=tuple(o[0] for o in outs),
        scratch_shapes=[pltpu.VMEM((tm, D_MODEL), BF16)],
        compiler_params=pltpu.CompilerParams(
            dimension_semantics=("arbitrary",),
            vmem_limit_bytes=_vmem_limit(pipelined, resident)),
        name="proj",
    )(x2d, g_norm.astype(F32)[None, :], w_groups_bf, wkt_bf, bd, gq, gkt, cos_t, s1_t, s2_t,
      cos_tt, sin_tt, b_glu.astype(F32)[None, :])


def _lambda_full(lq1_ref, lk1_ref, lq2_ref, lk2_ref, lam_init):
    a = jnp.sum(lq1_ref[...] * lk1_ref[...], axis=-1, keepdims=True)
    b = jnp.sum(lq2_ref[...] * lk2_ref[...], axis=-1, keepdims=True)
    return jnp.exp(a) - jnp.exp(b) + lam_init


def _subnorm_gate(o, gsub, gate, lam_init):
    ms = jnp.mean(o * o, axis=-1, keepdims=True)
    y = o * lax.rsqrt(ms + 1e-5) * gsub
    return y * (1.0 - lam_init) * gate


def _prompt_attn_kernel(lq1_ref, lk1_ref, lq2_ref, lk2_ref, gsub_ref, q_ref, kt_ref, v_ref, ga_ref,
                        o_ref, m_scr, l_scr, acc_scr, s_scr, *, tq, tk, hp, lam_init):
    qi = pl.program_id(2)
    lane = lax.broadcasted_iota(jnp.int32, (tq, LANES), 1)
    chains = []
    for h in range(hp):
        qh = q_ref[0, :, h * LANES:(h + 1) * LANES]
        zero = jnp.zeros_like(qh)
        chains.append((h, jnp.where(lane < QK_DIM, qh, zero)))
        chains.append((h, jnp.where(lane >= QK_DIM, qh, zero)))
    m_scr[...] = jnp.full(m_scr.shape, -jnp.inf, F32)
    l_scr[...] = jnp.zeros(l_scr.shape, F32)
    acc_scr[...] = jnp.zeros(acc_scr.shape, F32)
    n_lane_tiles = tk // LANES

    def scores(c, j):
        h, qc = chains[c]
        start = pl.multiple_of(j * tk, tk)
        kt = kt_ref[0, h * LANES:(h + 1) * LANES, pl.ds(start, tk)]
        return jnp.dot(qc, kt, preferred_element_type=F32)

    def step(j, masked, prefetch):
        start = pl.multiple_of(j * tk, tk)
        if masked:
            row = qi * tq + lax.broadcasted_iota(jnp.int32, (tq, tk), 0)
            col = start + lax.broadcasted_iota(jnp.int32, (tq, tk), 1)
            visible = col <= row
        for c, (h, _) in enumerate(chains):
            vt = v_ref[0, pl.ds(start, tk), h * LANES:(h + 1) * LANES]
            s = s_scr[c]
            if prefetch:
                s_scr[c] = scores(c, j + 1)
            if masked:
                s = jnp.where(visible, s, -jnp.inf)
            tiles = [s[:, t * LANES:(t + 1) * LANES] for t in range(n_lane_tiles)]
            tile_max = functools.reduce(jnp.maximum, tiles)
            m_old = m_scr[c]
            m_new = jnp.maximum(m_old, jnp.max(tile_max, axis=1, keepdims=True))
            alpha = jnp.exp2(m_old - m_new)
            ps = [jnp.exp2(t - m_new) for t in tiles]
            l_scr[c] = alpha * l_scr[c] + functools.reduce(jnp.add, ps)
            pb = jnp.concatenate([p.astype(BF16) for p in ps], axis=1)
            acc_scr[c] = alpha * acc_scr[c] + jnp.dot(pb, vt, preferred_element_type=F32)
            m_scr[c] = m_new

    def body(j, carry):
        step(j, False, True)
        return carry

    tiles_per_q = tq // tk
    for c in range(len(chains)):
        s_scr[c] = scores(c, 0)
    lax.fori_loop(0, qi * tiles_per_q, body, 0)
    for jj in range(tiles_per_q):
        step(qi * tiles_per_q + jj, True, jj + 1 < tiles_per_q)

    lam = _lambda_full(lq1_ref, lk1_ref, lq2_ref, lk2_ref, lam_init)
    for h in range(hp):
        o1 = acc_scr[2 * h] / jnp.sum(l_scr[2 * h], axis=1, keepdims=True)
        o2 = acc_scr[2 * h + 1] / jnp.sum(l_scr[2 * h + 1], axis=1, keepdims=True)
        gate = ga_ref[0, :, h * LANES:(h + 1) * LANES].astype(F32)
        y = _subnorm_gate(o1 - lam * o2, gsub_ref[...], gate, lam_init)
        o_ref[0, :, h * LANES:(h + 1) * LANES] = y.astype(BF16)


def _prompt_attn(q, kbt, vb, ga, lams, g_sub, lam_init):
    b, t, _ = q.shape
    tq = min(ATTN_ROWS, t)
    tk = min(ATTN_KEYS, tq)
    hp = ATTN_HEADS_PER_STEP
    assert t % tq == 0 and tq % tk == 0 and ATTN_HEADS % hp == 0
    width = hp * LANES
    small = pl.BlockSpec((1, QK_DIM), lambda bi, h, qi: (0, 0))
    tile = pl.BlockSpec((1, tq, width), lambda bi, h, qi: (bi, qi, h))
    keys = pl.BlockSpec((1, width, t), lambda bi, h, qi: (bi, h, 0))
    vals = pl.BlockSpec((1, t, width), lambda bi, h, qi: (bi, 0, h))
    state = (2 * hp, tq, LANES)
    pipelined = 3 * _nbytes((tq, width), BF16) + 2 * _nbytes((t, width), BF16)
    score_shape = (2 * hp, tq, tk)
    resident = 3 * _nbytes(state, F32) + 3 * _nbytes(score_shape, F32)
    return pl.pallas_call(
        functools.partial(_prompt_attn_kernel, tq=tq, tk=tk, hp=hp, lam_init=lam_init),
        grid=(b, ATTN_HEADS // hp, t // tq),
        in_specs=[small, small, small, small,
                  pl.BlockSpec((1, V_DIM), lambda bi, h, qi: (0, 0)),
                  tile, keys, vals, tile],
        out_specs=tile,
        out_shape=jax.ShapeDtypeStruct((b, t, D_ATTN), BF16),
        scratch_shapes=[pltpu.VMEM(state, F32), pltpu.VMEM(state, F32), pltpu.VMEM(state, F32),
                        pltpu.VMEM(score_shape, F32)],
        compiler_params=pltpu.CompilerParams(
            dimension_semantics=("arbitrary", "arbitrary", "arbitrary"),
            vmem_limit_bytes=_vmem_limit(pipelined, resident)),
        name="prompt_attn",
    )(*lams, g_sub.astype(F32)[None, :], q, kbt, vb, ga)


def _decode_attn_kernel(pt_ref, lq1_ref, lk1_ref, lq2_ref, lk2_ref, gsub_ref, qc_ref, ktn_ref, vn_ref,
                        ga_ref, *refs, n_pages_step, n_q, lam_init):
    k_refs = refs[:n_pages_step]
    v_refs = refs[n_pages_step:2 * n_pages_step]
    o_ref, m_scr, l_scr, acc_scr = refs[2 * n_pages_step:]
    del pt_ref
    seq = pl.program_id(0)
    s_idx = pl.program_id(1)
    rows_pass = HEADS_PER_PASS * n_q
    rows_h = 2 * n_q

    @pl.when(s_idx == 0)
    def _():
        m_scr[...] = jnp.full(m_scr.shape, -jnp.inf, F32)
        l_scr[...] = jnp.zeros(l_scr.shape, F32)
        acc_scr[...] = jnp.zeros(acc_scr.shape, F32)

    def scores(kt):
        return jnp.concatenate(
            [jnp.dot(qc_ref[0, c * rows_pass:(c + 1) * rows_pass, :],
                     kt[c * MXU_DEPTH:(c + 1) * MXU_DEPTH, :], preferred_element_type=F32)
             for c in range(N_PASSES)], axis=0)

    def update(s, values_of_head, l_lanes):
        m_old = m_scr[...]
        m_new = jnp.maximum(m_old, jnp.max(s, axis=1, keepdims=True))
        alpha = jnp.exp2(m_old - m_new)
        p = jnp.exp2(s - m_new)
        p_sum = p[:, 0:l_lanes]
        for t in range(1, s.shape[1] // l_lanes):
            p_sum = p_sum + p[:, t * l_lanes:(t + 1) * l_lanes]
        if l_lanes < LANES:
            p_sum = jnp.concatenate([p_sum, jnp.zeros((s.shape[0], LANES - l_lanes), F32)], axis=1)
        l_scr[...] = alpha * l_scr[...] + p_sum
        pb = p.astype(BF16)
        pv = jnp.concatenate(
            [jnp.dot(pb[h * rows_h:(h + 1) * rows_h, :], values_of_head(h),
                     preferred_element_type=F32) for h in range(ATTN_HEADS)], axis=0)
        acc_scr[...] = alpha * acc_scr[...] + pv
        m_scr[...] = m_new

    s_pages = jnp.concatenate([scores(k_refs[g][0].astype(BF16)) for g in range(n_pages_step)],
                              axis=1)

    def cached_values(h):
        return jnp.concatenate(
            [v_refs[g][0, pl.ds(h, PAGE_SIZE, stride=ATTN_HEADS), :].astype(BF16)
             for g in range(n_pages_step)], axis=0)

    update(s_pages, cached_values, LANES)

    @pl.when(s_idx == pl.num_programs(1) - 1)
    def _():
        s = scores(ktn_ref[...])
        n_new = s.shape[1]
        tok = lax.broadcasted_iota(jnp.int32, s.shape, 1)
        qry = lax.broadcasted_iota(jnp.int32, s.shape, 0) % n_q
        lo = seq * n_q
        visible = (tok >= lo) & (tok <= lo + qry)
        update(jnp.where(visible, s, -jnp.inf),
               lambda h: vn_ref[:, h * V_DIM:(h + 1) * V_DIM], min(n_new, LANES))

        lam = _lambda_full(lq1_ref, lk1_ref, lq2_ref, lk2_ref, lam_init)
        o_all = acc_scr[...] / jnp.sum(l_scr[...], axis=1, keepdims=True)
        for h in range(ATTN_HEADS):
            o = (o_all[h * rows_h:h * rows_h + n_q]
                 - lam * o_all[h * rows_h + n_q:(h + 1) * rows_h])
            gate = ga_ref[:, h * V_DIM:(h + 1) * V_DIM]
            o_ref[:, h * V_DIM:(h + 1) * V_DIM] = _subnorm_gate(o, gsub_ref[...], gate, lam_init)


def _decode_attn(q, ktn, vn, ga, cache_kt, cache_v, page_table, lams, g_sub, lam_init):
    bd, s, _ = q.shape
    n_pages = page_table.shape[1]
    g = min(DEC_PAGES, n_pages)
    assert n_pages % g == 0 and s == SUBLANES
    n_rows = N_QK_HEADS * s
    q6 = q.reshape(bd, s, N_PASSES, HEADS_PER_PASS, QK_DIM)
    eye = jnp.eye(HEADS_PER_PASS, dtype=q.dtype)
    qc = jnp.einsum('bqchd,hg->bchqgd', q6, eye).reshape(bd, n_rows, MXU_DEPTH)

    small = pl.BlockSpec((1, QK_DIM), lambda b, i, pt: (0, 0))
    const = lambda shape: pl.BlockSpec(shape, lambda b, i, pt: (0, 0))
    per_seq = pl.BlockSpec((s, D_ATTN), lambda b, i, pt: (b, 0))

    def page_spec(gi):
        return pl.BlockSpec((1, D_ATTN, PAGE_SIZE), lambda b, i, pt: (pt[b, i * g + gi], 0, 0))

    in_specs = ([small, small, small, small, const((1, V_DIM)),
                 pl.BlockSpec((1, n_rows, MXU_DEPTH), lambda b, i, pt: (b, 0, 0)),
                 const((D_ATTN, bd * s)), const((bd * s, D_ATTN)), per_seq]
                + [page_spec(gi) for gi in range(g)] + [page_spec(gi) for gi in range(g)])
    pipelined = (2 * g * _nbytes((PAGE_SIZE, D_ATTN), F32) + _nbytes((n_rows, MXU_DEPTH), BF16)
                 + 2 * _nbytes((bd * s, D_ATTN), BF16) + 2 * _nbytes((s, D_ATTN), F32))
    resident = (2 * _nbytes((g * PAGE_SIZE, D_ATTN), BF16) + 3 * _nbytes((n_rows, LANES), F32)
                + 4 * _nbytes((n_rows, g * PAGE_SIZE), F32))
    grid_spec = pltpu.PrefetchScalarGridSpec(
        num_scalar_prefetch=1,
        grid=(bd, n_pages // g),
        in_specs=in_specs,
        out_specs=pl.BlockSpec((s, D_ATTN), lambda b, i, pt: (b, 0)),
        scratch_shapes=[pltpu.VMEM((n_rows, 1), F32), pltpu.VMEM((n_rows, LANES), F32),
                        pltpu.VMEM((n_rows, V_DIM), F32)],
    )
    return pl.pallas_call(
        functools.partial(_decode_attn_kernel, n_pages_step=g, n_q=s, lam_init=lam_init),
        grid_spec=grid_spec,
        out_shape=jax.ShapeDtypeStruct((bd * s, D_ATTN), F32),
        compiler_params=pltpu.CompilerParams(
            dimension_semantics=("arbitrary", "arbitrary"),
            vmem_limit_bytes=_vmem_limit(pipelined, resident)),
        name="decode_attn",
    )(page_table, *lams, g_sub.astype(F32)[None, :], qc, ktn, vn, ga,
      *([cache_kt] * g), *([cache_v] * g))


def _tail_kernel(ctx_ref, c_ref, gc_ref, ma_ref, x_ref, wdw_ref, bdw_ref, gln_ref, bln_ref,
                 wpw_ref, bpw_ref, wout_ref, y_ref, ext_scr, sh_scr, cv_scr, *, bs, tt, rc, zero_first):
    i = pl.program_id(1)
    sh_rows = tt + CTX_ROWS - SUBLANES
    for b in range(bs):
        ctx = ctx_ref[b]
        if zero_first:
            ctx = jnp.where(i == 0, jnp.zeros_like(ctx), ctx)
        ext_scr[0:CTX_ROWS, :] = ctx
        ext_scr[CTX_ROWS:CTX_ROWS + tt, :] = c_ref[b * tt:(b + 1) * tt, :]
        for s in range(1, SUBLANES):
            sh_scr[s - 1, 0:sh_rows, :] = ext_scr[s:s + sh_rows, :]

        def chunk(ci, carry, b=b):
            r0 = pl.multiple_of(ci * rc, rc)
            acc = jnp.broadcast_to(bdw_ref[...], (rc, D_CONV))
            for tap in range(CONV_WIDTH):
                whole, s = divmod(CTX_PAD + tap, SUBLANES)
                rows = pl.ds(r0 + whole * SUBLANES, rc)
                win = ext_scr[rows, :] if s == 0 else sh_scr[s - 1, rows, :]
                acc = acc + win * wdw_ref[tap:tap + 1, :]
            cv_scr[pl.ds(b * tt + r0, rc), :] = acc
            return carry

        lax.fori_loop(0, tt // rc, chunk, 0)

    cv = cv_scr[...]
    mu = jnp.mean(cv, axis=-1, keepdims=True)
    d = cv - mu
    var = jnp.mean(d * d, axis=-1, keepdims=True)
    yn = d * lax.rsqrt(var + 1e-5) * gln_ref[...] + bln_ref[...]
    act = _silu(yn).astype(BF16)
    pw = jnp.dot(act, wpw_ref[...], preferred_element_type=F32) + bpw_ref[...]
    mix_c = (pw * gc_ref[...].astype(F32)).astype(BF16)
    mix_a = ma_ref[...].astype(BF16)
    y_ref[...] = (x_ref[...]
                  + jnp.dot(mix_a, wout_ref[0:D_ATTN, :], preferred_element_type=F32)
                  + jnp.dot(mix_c, wout_ref[D_ATTN:, :], preferred_element_type=F32))


def _tail(ctx, ctx_map, c, gc, mix_a, x, w_dw, b_dw, g_ln, b_ln, w_pw_bf, b_pw2, w_out_bf,
          *, b, t, bs, tt, zero_first):
    assert b % bs == 0 and t % tt == 0 and (bs == 1 or tt == t)
    rc = min(CONV_CHUNK, tt)
    assert tt % rc == 0
    tiles_per_seq = t // tt
    tile = lambda width: pl.BlockSpec((bs * tt, width), lambda bi, i: (bi * tiles_per_seq + i, 0))
    const = lambda shape: _resident(shape, lambda bi, i: (0, 0))
    in_specs = [pl.BlockSpec((bs, CTX_ROWS, D_CONV), ctx_map),
                tile(D_CONV), tile(D_CONV), tile(D_ATTN), tile(D_MODEL),
                const((CONV_WIDTH, D_CONV)), const((1, D_CONV)), const((1, D_CONV)),
                const((1, D_CONV)), const((D_CONV, D_CONV)), const((1, D_CONV)),
                const((D_MODEL, D_MODEL))]
    rows = bs * tt
    ext_shape = (CTX_ROWS + tt, D_CONV)
    sh_shape = (SUBLANES - 1,) + ext_shape
    pipelined = (_nbytes((bs, CTX_ROWS, D_CONV), F32) + _nbytes((rows, D_CONV), F32)
                 + _nbytes((rows, D_CONV), gc.dtype) + _nbytes((rows, D_ATTN), mix_a.dtype)
                 + 2 * _nbytes((rows, D_MODEL), F32))
    resident = (_nbytes((CTX_ROWS, D_CONV), F32) + _nbytes((D_CONV, D_CONV), BF16)
                + _nbytes((D_MODEL, D_MODEL), BF16) + _nbytes(ext_shape, F32)
                + _nbytes(sh_shape, F32) + 4 * _nbytes((rows, D_CONV), F32))
    return pl.pallas_call(
        functools.partial(_tail_kernel, bs=bs, tt=tt, rc=rc, zero_first=zero_first),
        grid=(b // bs, tiles_per_seq),
        in_specs=in_specs,
        out_specs=tile(D_MODEL),
        out_shape=jax.ShapeDtypeStruct((b * t, D_MODEL), F32),
        scratch_shapes=[pltpu.VMEM(ext_shape, F32), pltpu.VMEM(sh_shape, F32),
                        pltpu.VMEM((rows, D_CONV), F32)],
        compiler_params=pltpu.CompilerParams(
            dimension_semantics=("arbitrary", "arbitrary"),
            vmem_limit_bytes=_vmem_limit(pipelined, resident)),
        name="tail",
    )(ctx, c, gc, mix_a, x, w_dw.astype(F32), b_dw.astype(F32)[None, :], g_ln.astype(F32)[None, :],
      b_ln.astype(F32)[None, :], w_pw_bf, b_pw2.astype(F32)[None, :], w_out_bf)


def kernel(x_prompt, x_sample, cache_k, cache_v, state_conv, page_table, g_norm, w_in, g_q, g_k, lambda_q1, lambda_k1, lambda_q2, lambda_k2, g_sub, b_glu, w_dw, b_dw, g_ln, b_ln, w_pw2, b_pw2, w_out):
    bp, tp, _ = x_prompt.shape
    bs, ts, _ = x_sample.shape
    depth = w_in.shape[0]
    n_pool = cache_k.shape[1]
    past = page_table.shape[1] * PAGE_SIZE
    pos_p = jnp.arange(tp)
    pos_s = jnp.tile(past + jnp.arange(ts), bs)
    tt_p = min(TAIL_ROWS, tp)
    seqs = min(DEC_SEQS, bs)
    mp, ms = bp * tp, bs * ts

    yp, ys = x_prompt.reshape(mp, D_MODEL), x_sample.reshape(ms, D_MODEL)
    outs = [[] for _ in range(6)]
    for l in range(depth):
        lam_init = 0.8 - 0.6 * math.exp(-0.3 * l)
        w_l = w_in[l].astype(BF16)
        w_groups = jnp.concatenate([w_l[:, :D_ATTN], w_l[:, 2 * D_ATTN:]], axis=1)
        wkt = w_l[:, D_ATTN:2 * D_ATTN].T
        w_pw_bf = w_pw2[l].astype(BF16)
        w_out_bf = w_out[l].astype(BF16)
        lams = tuple(v[l].astype(F32)[None, :] for v in (lambda_q1, lambda_k1, lambda_q2, lambda_k2))
        tail_w = (w_dw[l], b_dw[l], g_ln[l], b_ln[l], w_pw_bf, b_pw2[l], w_out_bf)
        proj_w = (g_norm[l], w_groups, wkt, g_q[l], g_k[l], b_glu[l])

        q, kt, kbt, v, vb, ga, c, gc = _proj(yp, pos_p, *proj_w)
        as3 = lambda a: a.reshape(bp, tp, D_ATTN)
        mix_a = _prompt_attn(as3(q), kbt, as3(vb), as3(ga), lams, g_sub[l], lam_init)
        c3 = as3(c)
        blocks_per_tile = tt_p // CTX_ROWS
        prev_block = lambda bi, i: (bi, jnp.maximum(i * blocks_per_tile - 1, 0), 0)
        yp = _tail(c3, prev_block, c, gc, mix_a.reshape(mp, D_ATTN), yp, *tail_w,
                   b=bp, t=tp, bs=1, tt=tt_p, zero_first=True)
        k_rows = kt.reshape(bp, N_QK_HEADS, QK_DIM, tp).transpose(0, 3, 1, 2)
        outs[0].append(k_rows)
        outs[1].append(v.reshape(bp, tp, ATTN_HEADS, V_DIM))
        outs[2].append(c3[:, tp - CONV_CTX:])

        q, kt, kbt, v, vb, ga, c, gc = _proj(ys, pos_s, *proj_w)
        cache_kt = cache_k[l].transpose(0, 2, 3, 1).reshape(n_pool, D_ATTN, PAGE_SIZE)
        cache_vr = cache_v[l].reshape(n_pool, PAGE_SIZE * ATTN_HEADS, V_DIM)
        mix_a = _decode_attn(q.reshape(bs, ts, D_ATTN), kbt[0], vb, ga.astype(F32),
                             cache_kt, cache_vr, page_table, lams, g_sub[l], lam_init)
        ctx = jnp.pad(state_conv[l].astype(F32), ((0, 0), (CTX_PAD, 0), (0, 0)))
        ys = _tail(ctx, lambda bi, i: (bi, 0, 0), c, gc, mix_a, ys, *tail_w,
                   b=bs, t=ts, bs=seqs, tt=ts, zero_first=False)
        k_rows = kt[0].T.reshape(bs, ts, N_QK_HEADS, QK_DIM)
        outs[3].append(k_rows)
        outs[4].append(v.reshape(bs, ts, ATTN_HEADS, V_DIM))
        c3 = c.reshape(bs, ts, D_CONV)
        outs[5].append(jnp.concatenate([state_conv[l].astype(F32), c3], axis=1)[:, -CONV_CTX:])

    return ((yp.reshape(bp, tp, D_MODEL), ys.reshape(bs, ts, D_MODEL))
            + tuple(jnp.stack(o) for o in outs))
```

```python
import functools
import math

import jax
import jax.numpy as jnp
from jax import lax
from jax.experimental import pallas as pl
from jax.experimental.pallas import tpu as pltpu

D_MODEL = 2048
D_ATTN = D_MODEL // 2
D_CONV = D_MODEL - D_ATTN
ATTN_HEADS = 8
N_QK_HEADS = 2 * ATTN_HEADS
QK_DIM = D_ATTN // N_QK_HEADS
V_DIM = 2 * QK_DIM
ROT_DIM = QK_DIM // 4
ROT_HALF = ROT_DIM // 2
ROPE_THETA = 500000.0
CONV_WIDTH = 31
CONV_CTX = CONV_WIDTH - 1
PAGE_SIZE = 128
SCALE = QK_DIM ** -0.5
Q_SCALE = SCALE * math.log2(math.e)
G_Q, G_K, G_V, G_GATE_A, G_GLU_A, G_GLU_B, G_GATE_C = range(7)
D_IN = 7 * D_ATTN

LANES = 128
SUBLANES = 8
MXU_DEPTH = 256
CTX_ROWS = 32
CTX_PAD = CTX_ROWS - CONV_CTX
VMEM_INTERNAL_BYTES = 12 << 20

PROJ_ROWS = 256
ATTN_ROWS = 512
ATTN_KEYS = 512
ATTN_HEADS_PER_STEP = 1
TAIL_ROWS = 256
CONV_CHUNK = 32
DEC_PAGES = 16
DEC_SEQS = 8

HEADS_PER_PASS = MXU_DEPTH // QK_DIM
N_PASSES = N_QK_HEADS // HEADS_PER_PASS

F32 = jnp.float32
BF16 = jnp.bfloat16


def _vmem_limit(pipelined_bytes, resident_bytes):
    return int(2 * pipelined_bytes + resident_bytes + VMEM_INTERNAL_BYTES)


def _nbytes(shape, dtype):
    return math.prod(shape) * jnp.dtype(dtype).itemsize


def _sigmoid(x):
    return 1.0 / (1.0 + jnp.exp(-x))


def _silu(x):
    return x * _sigmoid(x)


def _resident(shape, index_map):
    return pl.BlockSpec(shape, index_map, pipeline_mode=pl.Buffered(1))


def _proj_kernel(x_ref, gn_ref, w_ref, bd_ref, gq_ref, gkt_ref, cos_ref, s1_ref, s2_ref,
                 cost_ref, sint_ref, bglu_ref,
                 q_ref, kt_ref, kbt_ref, v_ref, vb_ref, ga_ref, c_ref, gc_ref, h_scr, zk_scr):
    x = x_ref[...]
    ms = jnp.mean(x * x, axis=-1, keepdims=True)
    h_scr[...] = (x * lax.rsqrt(ms + 1e-6) * gn_ref[...]).astype(BF16)

    def group(g):
        return jnp.dot(h_scr[...], w_ref[:, g * D_ATTN:(g + 1) * D_ATTN],
                       preferred_element_type=F32)

    zq = group(G_Q)
    for c in range(D_ATTN // LANES):
        zc = zq[:, c * LANES:(c + 1) * LANES]
        ss = jnp.dot((zc * zc).astype(BF16), bd_ref[...], preferred_element_type=F32)
        y = zc * lax.rsqrt(ss * (1.0 / QK_DIM) + 1e-6) * gq_ref[...]
        y = (y * cos_ref[...]
             + pltpu.roll(y, LANES - ROT_HALF, 1) * s1_ref[...]
             + pltpu.roll(y, ROT_HALF, 1) * s2_ref[...])
        q_ref[:, c * LANES:(c + 1) * LANES] = (y * Q_SCALE).astype(BF16)

    zk_scr[...] = group(G_K)
    zkt = zk_scr[...].T
    for hd in range(N_QK_HEADS):
        blk = zkt[hd * QK_DIM:(hd + 1) * QK_DIM, :]
        ss = jnp.sum(blk * blk, axis=0, keepdims=True)
        y = blk * lax.rsqrt(ss * (1.0 / QK_DIM) + 1e-6) * gkt_ref[...]
        x1, x2 = y[0:ROT_HALF], y[ROT_HALF:ROT_DIM]
        cos, sin = cost_ref[...], sint_ref[...]
        y = jnp.concatenate([x1 * cos - x2 * sin, x2 * cos + x1 * sin, y[ROT_DIM:]], axis=0)
        kt_ref[0, hd * QK_DIM:(hd + 1) * QK_DIM, :] = y
        kbt_ref[0, hd * QK_DIM:(hd + 1) * QK_DIM, :] = y.astype(BF16)

    zv = group(G_V)
    v_ref[...] = zv
    vb_ref[...] = zv.astype(BF16)
    ga_ref[...] = _silu(group(G_GATE_A)).astype(BF16)
    u = group(G_GLU_A) + bglu_ref[:, :D_CONV]
    c_ref[...] = u * _sigmoid(group(G_GLU_B) + bglu_ref[:, D_CONV:])
    gc_ref[...] = _silu(group(G_GATE_C)).astype(BF16)


def _rope_tables(pos):
    inv = jnp.power(ROPE_THETA, -jnp.arange(ROT_HALF, dtype=F32) * (2.0 / ROT_DIM))
    ang = pos.astype(F32)[:, None] * inv[None, :]
    cos, sin = jnp.cos(ang), jnp.sin(ang)
    n = pos.shape[0]
    rest = QK_DIM - ROT_DIM
    zeros = jnp.zeros((n, ROT_HALF), F32)
    cos_t = jnp.concatenate([cos, cos, jnp.ones((n, rest), F32)], axis=1)
    s1_t = jnp.concatenate([-sin, zeros, jnp.zeros((n, rest), F32)], axis=1)
    s2_t = jnp.concatenate([zeros, sin, jnp.zeros((n, rest), F32)], axis=1)
    rep = LANES // QK_DIM
    rows = tuple(jnp.tile(t, (1, rep)) for t in (cos_t, s1_t, s2_t))
    return rows + (cos.T, sin.T)


def _proj(x2d, pos, g_norm, w_in_bf, g_q, g_k, b_glu):
    m = x2d.shape[0]
    p = pos.shape[0]
    tm = min(PROJ_ROWS, m)
    assert m % p == 0 and p % tm == 0
    n_pos_blocks = p // tm
    cos_t, s1_t, s2_t, cos_tt, sin_tt = _rope_tables(pos)
    head = jnp.arange(LANES) // QK_DIM
    bd = (head[:, None] == head[None, :]).astype(BF16)
    gq = jnp.tile(g_q.astype(F32), LANES // QK_DIM)[None, :]
    gkt = jnp.broadcast_to(g_k.astype(F32)[:, None], (QK_DIM, tm))

    row = lambda i: (i, 0)
    const = lambda i: (0, 0)
    tab = lambda i: (i % n_pos_blocks, 0)
    tab_t = lambda i: (0, i % n_pos_blocks)
    kt_map = lambda i: (i // n_pos_blocks, 0, i % n_pos_blocks)
    in_specs = [
        pl.BlockSpec((tm, D_MODEL), row),
        _resident((1, D_MODEL), const),
        _resident((D_MODEL, D_IN), const),
        _resident((LANES, LANES), const),
        _resident((1, LANES), const),
        _resident((QK_DIM, tm), const),
        pl.BlockSpec((tm, LANES), tab),
        pl.BlockSpec((tm, LANES), tab),
        pl.BlockSpec((tm, LANES), tab),
        pl.BlockSpec((ROT_HALF, tm), tab_t),
        pl.BlockSpec((ROT_HALF, tm), tab_t),
        _resident((1, 2 * D_CONV), const),
    ]
    rows_out = lambda dt: (jax.ShapeDtypeStruct((m, D_ATTN), dt), pl.BlockSpec((tm, D_ATTN), row))
    kt_out = lambda dt: (jax.ShapeDtypeStruct((m // p, D_ATTN, p), dt),
                         pl.BlockSpec((1, D_ATTN, tm), kt_map))
    outs = [rows_out(BF16), kt_out(F32), kt_out(BF16), rows_out(F32), rows_out(BF16),
            rows_out(BF16), rows_out(F32), rows_out(BF16)]
    pipelined = (_nbytes((tm, D_MODEL), F32) + 3 * _nbytes((tm, LANES), F32)
                 + sum(_nbytes((tm, D_ATTN), o[0].dtype) for o in outs))
    resident = (_nbytes((D_MODEL, D_IN), BF16) + _nbytes((tm, D_MODEL), BF16)
                + 4 * _nbytes((tm, D_ATTN), F32))
    return pl.pallas_call(
        _proj_kernel,
        grid=(m // tm,),
        in_specs=in_specs,
        out_specs=tuple(o[1] for o in outs),
        out_shape=tuple(o[0] for o in outs),
        scratch_shapes=[pltpu.VMEM((tm, D_MODEL), BF16), pltpu.VMEM((tm, D_ATTN), F32)],
        compiler_params=pltpu.CompilerParams(
            dimension_semantics=("arbitrary",),
            vmem_limit_bytes=_vmem_limit(pipelined, resident)),
        name="proj",
    )(x2d, g_norm.astype(F32)[None, :], w_in_bf, bd, gq, gkt, cos_t, s1_t, s2_t,
      cos_tt, sin_tt, b_glu.astype(F32)[None, :])


def _lambda_full(lq1_ref, lk1_ref, lq2_ref, lk2_ref, lam_init):
    a = jnp.sum(lq1_ref[...] * lk1_ref[...], axis=-1, keepdims=True)
    b = jnp.sum(lq2_ref[...] * lk2_ref[...], axis=-1, keepdims=True)
    return jnp.exp(a) - jnp.exp(b) + lam_init


def _subnorm_gate(o, gsub, gate, lam_init):
    ms = jnp.mean(o * o, axis=-1, keepdims=True)
    y = o * lax.rsqrt(ms + 1e-5) * gsub
    return y * (1.0 - lam_init) * gate


def _prompt_tile(qi, lam, gsub_ref, q_ref, kt_ref, v_ref, ga_ref, o_ref, m_scr, l_scr, acc_scr, s_scr,
                 *, tq, tk, hp, lam_init):
    lane = lax.broadcasted_iota(jnp.int32, (tq, LANES), 1)
    chains = []
    for h in range(hp):
        qh = q_ref[0, :, h * LANES:(h + 1) * LANES]
        zero = jnp.zeros_like(qh)
        chains.append((h, jnp.where(lane < QK_DIM, qh, zero)))
        chains.append((h, jnp.where(lane >= QK_DIM, qh, zero)))
    m_scr[...] = jnp.full(m_scr.shape, -jnp.inf, F32)
    l_scr[...] = jnp.zeros(l_scr.shape, F32)
    acc_scr[...] = jnp.zeros(acc_scr.shape, F32)
    n_lane_tiles = tk // LANES

    def scores(c, j):
        h, qc = chains[c]
        start = pl.multiple_of(j * tk, tk)
        kt = kt_ref[0, h * LANES:(h + 1) * LANES, pl.ds(start, tk)]
        return jnp.dot(qc, kt, preferred_element_type=F32)

    def step(j, masked, prefetch):
        start = pl.multiple_of(j * tk, tk)
        if masked:
            row = qi * tq + lax.broadcasted_iota(jnp.int32, (tq, tk), 0)
            col = start + lax.broadcasted_iota(jnp.int32, (tq, tk), 1)
            visible = col <= row
        for c, (h, _) in enumerate(chains):
            vt = v_ref[0, pl.ds(start, tk), h * LANES:(h + 1) * LANES]
            s = s_scr[c]
            if prefetch:
                s_scr[c] = scores(c, j + 1)
            if masked:
                s = jnp.where(visible, s, -jnp.inf)
            tiles = [s[:, t * LANES:(t + 1) * LANES] for t in range(n_lane_tiles)]
            tile_max = functools.reduce(jnp.maximum, tiles)
            m_old = m_scr[c]
            m_new = jnp.maximum(m_old, jnp.max(tile_max, axis=1, keepdims=True))
            alpha = jnp.exp2(m_old - m_new)
            ps = [jnp.exp2(t - m_new) for t in tiles]
            l_scr[c] = alpha * l_scr[c] + functools.reduce(jnp.add, ps)
            pb = jnp.concatenate([p.astype(BF16) for p in ps], axis=1)
            acc_scr[c] = alpha * acc_scr[c] + jnp.dot(pb, vt, preferred_element_type=F32)
            m_scr[c] = m_new

    def body(j, carry):
        step(j, False, True)
        return carry

    tiles_per_q = tq // tk
    for c in range(len(chains)):
        s_scr[c] = scores(c, 0)
    lax.fori_loop(0, qi * tiles_per_q, body, 0)
    for jj in range(tiles_per_q):
        step(qi * tiles_per_q + jj, True, jj + 1 < tiles_per_q)

    for h in range(hp):
        o1 = acc_scr[2 * h] / jnp.sum(l_scr[2 * h], axis=1, keepdims=True)
        o2 = acc_scr[2 * h + 1] / jnp.sum(l_scr[2 * h + 1], axis=1, keepdims=True)
        gate = ga_ref[0, :, h * LANES:(h + 1) * LANES].astype(F32)
        y = _subnorm_gate(o1 - lam * o2, gsub_ref[...], gate, lam_init)
        o_ref[0, :, h * LANES:(h + 1) * LANES] = y.astype(BF16)


def _decode_slice(seq, s_idx, n_steps, lam, gsub_ref, qc_ref, ktn_ref, vn_ref, ga_ref, k_refs, v_refs,
                  o_ref, m_scr, l_scr, acc_scr, *, n_q, lam_init):
    n_pages_step = len(k_refs)
    rows_pass = HEADS_PER_PASS * n_q
    rows_h = 2 * n_q

    @pl.when(s_idx == 0)
    def _():
        m_scr[...] = jnp.full(m_scr.shape, -jnp.inf, F32)
        l_scr[...] = jnp.zeros(l_scr.shape, F32)
        acc_scr[...] = jnp.zeros(acc_scr.shape, F32)

    def scores(kt):
        return jnp.concatenate(
            [jnp.dot(qc_ref[0, c * rows_pass:(c + 1) * rows_pass, :],
                     kt[c * MXU_DEPTH:(c + 1) * MXU_DEPTH, :], preferred_element_type=F32)
             for c in range(N_PASSES)], axis=0)

    def update(s, values_of_head, l_lanes):
        m_old = m_scr[...]
        m_new = jnp.maximum(m_old, jnp.max(s, axis=1, keepdims=True))
        alpha = jnp.exp2(m_old - m_new)
        p = jnp.exp2(s - m_new)
        p_sum = p[:, 0:l_lanes]
        for t in range(1, s.shape[1] // l_lanes):
            p_sum = p_sum + p[:, t * l_lanes:(t + 1) * l_lanes]
        if l_lanes < LANES:
            p_sum = jnp.concatenate([p_sum, jnp.zeros((s.shape[0], LANES - l_lanes), F32)], axis=1)
        l_scr[...] = alpha * l_scr[...] + p_sum
        pb = p.astype(BF16)
        pv = jnp.concatenate(
            [jnp.dot(pb[h * rows_h:(h + 1) * rows_h, :], values_of_head(h),
                     preferred_element_type=F32) for h in range(ATTN_HEADS)], axis=0)
        acc_scr[...] = alpha * acc_scr[...] + pv
        m_scr[...] = m_new

    s_pages = jnp.concatenate([scores(k_refs[g][0].astype(BF16)) for g in range(n_pages_step)],
                              axis=1)

    def cached_values(h):
        return jnp.concatenate(
            [v_refs[g][0, pl.ds(h, PAGE_SIZE, stride=ATTN_HEADS), :].astype(BF16)
             for g in range(n_pages_step)], axis=0)

    update(s_pages, cached_values, LANES)

    @pl.when(s_idx == n_steps - 1)
    def _():
        s = scores(ktn_ref[...])
        n_new = s.shape[1]
        tok = lax.broadcasted_iota(jnp.int32, s.shape, 1)
        qry = lax.broadcasted_iota(jnp.int32, s.shape, 0) % n_q
        lo = seq * n_q
        visible = (tok >= lo) & (tok <= lo + qry)
        update(jnp.where(visible, s, -jnp.inf),
               lambda h: vn_ref[:, h * V_DIM:(h + 1) * V_DIM], min(n_new, LANES))

        o_all = acc_scr[...] / jnp.sum(l_scr[...], axis=1, keepdims=True)
        for h in range(ATTN_HEADS):
            o = (o_all[h * rows_h:h * rows_h + n_q]
                 - lam * o_all[h * rows_h + n_q:(h + 1) * rows_h])
            gate = ga_ref[:, h * V_DIM:(h + 1) * V_DIM]
            o_ref[:, h * V_DIM:(h + 1) * V_DIM] = _subnorm_gate(o, gsub_ref[...], gate, lam_init)


def _attn_kernel(pt_ref, lq1_ref, lk1_ref, lq2_ref, lk2_ref, gsub_ref,
                 q_ref, kt_ref, v_ref, gap_ref, qc_ref, ktn_ref, vn_ref, gas_ref, *refs,
                 n_pages_step, n_prompt_steps, n_decode_steps, nq, steps_per_seq,
                 tq, tk, hp, n_q, lam_init):
    del pt_ref
    k_refs = refs[:n_pages_step]
    v_refs = refs[n_pages_step:2 * n_pages_step]
    op_ref, od_ref, pm_scr, pl_scr, pacc_scr, ps_scr, dm_scr, dl_scr, dacc_scr = refs[2 * n_pages_step:]
    step = pl.program_id(0)
    lam = _lambda_full(lq1_ref, lk1_ref, lq2_ref, lk2_ref, lam_init)

    def prompt():
        _prompt_tile(step % nq, lam, gsub_ref, q_ref, kt_ref, v_ref, gap_ref, op_ref,
                     pm_scr, pl_scr, pacc_scr, ps_scr, tq=tq, tk=tk, hp=hp, lam_init=lam_init)

    def decode():
        _decode_slice(step // steps_per_seq, step % steps_per_seq, steps_per_seq, lam, gsub_ref,
                      qc_ref, ktn_ref, vn_ref, gas_ref, k_refs, v_refs, od_ref,
                      dm_scr, dl_scr, dacc_scr, n_q=n_q, lam_init=lam_init)

    n_steps = max(n_prompt_steps, n_decode_steps)
    prompt() if n_prompt_steps == n_steps else pl.when(step < n_prompt_steps)(prompt)
    decode() if n_decode_steps == n_steps else pl.when(step < n_decode_steps)(decode)


def _attention(q_p, kbt_p, vb_p, ga_p, q_s, ktn, vn, ga_s, cache_kt, cache_v, page_table,
               lams, g_sub, lam_init):
    b, t, _ = q_p.shape
    tq = min(ATTN_ROWS, t)
    tk = min(ATTN_KEYS, tq)
    hp = ATTN_HEADS_PER_STEP
    assert t % tq == 0 and tq % tk == 0 and ATTN_HEADS % hp == 0
    nq, nh = t // tq, ATTN_HEADS // hp
    n_prompt_steps = b * nh * nq
    width = hp * LANES

    bd, s, _ = q_s.shape
    n_pages = page_table.shape[1]
    g = min(DEC_PAGES, n_pages)
    assert n_pages % g == 0 and s == SUBLANES
    steps_per_seq = n_pages // g
    n_decode_steps = bd * steps_per_seq
    n_rows = N_QK_HEADS * s
    q6 = q_s.reshape(bd, s, N_PASSES, HEADS_PER_PASS, QK_DIM)
    eye = jnp.eye(HEADS_PER_PASS, dtype=q_s.dtype)
    qc = jnp.einsum('bqchd,hg->bchqgd', q6, eye).reshape(bd, n_rows, MXU_DEPTH)

    def prompt_idx(i):
        i = jnp.minimum(i, n_prompt_steps - 1)
        return i // (nh * nq), (i // nq) % nh, i % nq

    def decode_idx(i):
        i = jnp.minimum(i, n_decode_steps - 1)
        return i // steps_per_seq, i % steps_per_seq

    def p_tile(i, pt):
        bi, h, qi = prompt_idx(i)
        return bi, qi, h

    def p_keys(i, pt):
        bi, h, _ = prompt_idx(i)
        return bi, h, 0

    def p_vals(i, pt):
        bi, h, _ = prompt_idx(i)
        return bi, 0, h

    def page_spec(gi):
        def idx(i, pt):
            seq, grp = decode_idx(i)
            return pt[seq, grp * g + gi], 0, 0
        return pl.BlockSpec((1, D_ATTN, PAGE_SIZE), idx)

    const2 = lambda shape: pl.BlockSpec(shape, lambda i, pt: (0, 0))
    small = const2((1, QK_DIM))
    tile = pl.BlockSpec((1, tq, width), p_tile)
    per_seq = pl.BlockSpec((s, D_ATTN), lambda i, pt: (decode_idx(i)[0], 0))
    in_specs = ([small, small, small, small, const2((1, V_DIM)),
                 tile, pl.BlockSpec((1, width, t), p_keys), pl.BlockSpec((1, t, width), p_vals), tile,
                 pl.BlockSpec((1, n_rows, MXU_DEPTH), lambda i, pt: (decode_idx(i)[0], 0, 0)),
                 const2((D_ATTN, bd * s)), const2((bd * s, D_ATTN)), per_seq]
                + [page_spec(gi) for gi in range(g)] + [page_spec(gi) for gi in range(g)])
    p_state = (2 * hp, tq, LANES)
    p_scores = (2 * hp, tq, tk)
    pipelined = (3 * _nbytes((tq, width), BF16) + 2 * _nbytes((t, width), BF16)
                 + 2 * g * _nbytes((PAGE_SIZE, D_ATTN), F32) + _nbytes((n_rows, MXU_DEPTH), BF16)
                 + 2 * _nbytes((bd * s, D_ATTN), BF16) + 2 * _nbytes((s, D_ATTN), F32))
    resident = (3 * _nbytes(p_state, F32) + 3 * _nbytes(p_scores, F32)
                + 2 * _nbytes((g * PAGE_SIZE, D_ATTN), BF16) + 3 * _nbytes((n_rows, LANES), F32)
                + 4 * _nbytes((n_rows, g * PAGE_SIZE), F32))
    grid_spec = pltpu.PrefetchScalarGridSpec(
        num_scalar_prefetch=1,
        grid=(max(n_prompt_steps, n_decode_steps),),
        in_specs=in_specs,
        out_specs=(tile, per_seq),
        scratch_shapes=[pltpu.VMEM(p_state, F32), pltpu.VMEM(p_state, F32), pltpu.VMEM(p_state, F32),
                        pltpu.VMEM(p_scores, F32),
                        pltpu.VMEM((n_rows, 1), F32), pltpu.VMEM((n_rows, LANES), F32),
                        pltpu.VMEM((n_rows, V_DIM), F32)],
    )
    return pl.pallas_call(
        functools.partial(_attn_kernel, n_pages_step=g, n_prompt_steps=n_prompt_steps,
                          n_decode_steps=n_decode_steps, nq=nq, steps_per_seq=steps_per_seq,
                          tq=tq, tk=tk, hp=hp, n_q=s, lam_init=lam_init),
        grid_spec=grid_spec,
        out_shape=(jax.ShapeDtypeStruct((b, t, D_ATTN), BF16),
                   jax.ShapeDtypeStruct((bd * s, D_ATTN), F32)),
        compiler_params=pltpu.CompilerParams(
            dimension_semantics=("arbitrary",),
            vmem_limit_bytes=_vmem_limit(pipelined, resident)),
        name="attention",
    )(page_table, *lams, g_sub.astype(F32)[None, :], q_p, kbt_p, vb_p, ga_p, qc, ktn, vn, ga_s,
      *([cache_kt] * g), *([cache_v] * g))


def _tail_kernel(ctx_ref, c_ref, gc_ref, ma_ref, x_ref, wdw_ref, bdw_ref, gln_ref, bln_ref,
                 wpw_ref, bpw_ref, wout_ref, y_ref, ext_scr, sh_scr, cv_scr, *, bs, tt, rc, zero_first):
    i = pl.program_id(1)
    sh_rows = tt + CTX_ROWS - SUBLANES
    for b in range(bs):
        ctx = ctx_ref[b]
        if zero_first:
            ctx = jnp.where(i == 0, jnp.zeros_like(ctx), ctx)
        ext_scr[0:CTX_ROWS, :] = ctx
        ext_scr[CTX_ROWS:CTX_ROWS + tt, :] = c_ref[b * tt:(b + 1) * tt, :]
        for s in range(1, SUBLANES):
            sh_scr[s - 1, 0:sh_rows, :] = ext_scr[s:s + sh_rows, :]

        def chunk(ci, carry, b=b):
            r0 = pl.multiple_of(ci * rc, rc)
            acc = jnp.broadcast_to(bdw_ref[...], (rc, D_CONV))
            for tap in range(CONV_WIDTH):
                whole, s = divmod(CTX_PAD + tap, SUBLANES)
                rows = pl.ds(r0 + whole * SUBLANES, rc)
                win = ext_scr[rows, :] if s == 0 else sh_scr[s - 1, rows, :]
                acc = acc + win * jnp.tile(wdw_ref[tap], (rc // SUBLANES, 1))
            cv_scr[pl.ds(b * tt + r0, rc), :] = acc
            return carry

        lax.fori_loop(0, tt // rc, chunk, 0)

    cv = cv_scr[...]
    mu = jnp.mean(cv, axis=-1, keepdims=True)
    d = cv - mu
    var = jnp.mean(d * d, axis=-1, keepdims=True)
    yn = d * lax.rsqrt(var + 1e-5) * gln_ref[...] + bln_ref[...]
    act = _silu(yn).astype(BF16)
    pw = jnp.dot(act, wpw_ref[...], preferred_element_type=F32) + bpw_ref[...]
    mix_c = (pw * gc_ref[...].astype(F32)).astype(BF16)
    mix_a = ma_ref[...].astype(BF16)
    y_ref[...] = (x_ref[...]
                  + jnp.dot(mix_a, wout_ref[0:D_ATTN, :], preferred_element_type=F32)
                  + jnp.dot(mix_c, wout_ref[D_ATTN:, :], preferred_element_type=F32))


def _tail(ctx, ctx_map, c, gc, mix_a, x, w_dw, b_dw, g_ln, b_ln, w_pw_bf, b_pw2, w_out_bf,
          *, b, t, bs, tt, zero_first):
    assert b % bs == 0 and t % tt == 0 and (bs == 1 or tt == t)
    rc = min(CONV_CHUNK, tt)
    assert tt % rc == 0
    tiles_per_seq = t // tt
    tile = lambda width: pl.BlockSpec((bs * tt, width), lambda bi, i: (bi * tiles_per_seq + i, 0))
    const = lambda shape: _resident(shape, lambda bi, i: (0,) * len(shape))
    in_specs = [pl.BlockSpec((bs, CTX_ROWS, D_CONV), ctx_map),
                tile(D_CONV), tile(D_CONV), tile(D_ATTN), tile(D_MODEL),
                const((CONV_WIDTH, SUBLANES, D_CONV)), const((1, D_CONV)), const((1, D_CONV)),
                const((1, D_CONV)), const((D_CONV, D_CONV)), const((1, D_CONV)),
                const((D_MODEL, D_MODEL))]
    rows = bs * tt
    ext_shape = (CTX_ROWS + tt, D_CONV)
    sh_shape = (SUBLANES - 1,) + ext_shape
    pipelined = (_nbytes((bs, CTX_ROWS, D_CONV), F32) + _nbytes((rows, D_CONV), F32)
                 + _nbytes((rows, D_CONV), gc.dtype) + _nbytes((rows, D_ATTN), mix_a.dtype)
                 + 2 * _nbytes((rows, D_MODEL), F32))
    resident = (_nbytes((CONV_WIDTH, SUBLANES, D_CONV), F32) + _nbytes((D_CONV, D_CONV), BF16)
                + _nbytes((D_MODEL, D_MODEL), BF16) + _nbytes(ext_shape, F32)
                + _nbytes(sh_shape, F32) + 4 * _nbytes((rows, D_CONV), F32))
    w_taps = jnp.broadcast_to(w_dw.astype(F32)[:, None, :], (CONV_WIDTH, SUBLANES, D_CONV))
    return pl.pallas_call(
        functools.partial(_tail_kernel, bs=bs, tt=tt, rc=rc, zero_first=zero_first),
        grid=(b // bs, tiles_per_seq),
        in_specs=in_specs,
        out_specs=tile(D_MODEL),
        out_shape=jax.ShapeDtypeStruct((b * t, D_MODEL), F32),
        scratch_shapes=[pltpu.VMEM(ext_shape, F32), pltpu.VMEM(sh_shape, F32),
                        pltpu.VMEM((rows, D_CONV), F32)],
        compiler_params=pltpu.CompilerParams(
            dimension_semantics=("arbitrary", "arbitrary"),
            vmem_limit_bytes=_vmem_limit(pipelined, resident)),
        name="tail",
    )(ctx, c, gc, mix_a, x, w_taps, b_dw.astype(F32)[None, :], g_ln.astype(F32)[None, :],
      b_ln.astype(F32)[None, :], w_pw_bf, b_pw2.astype(F32)[None, :], w_out_bf)


def kernel(x_prompt, x_sample, cache_k, cache_v, state_conv, page_table, g_norm, w_in, g_q, g_k, lambda_q1, lambda_k1, lambda_q2, lambda_k2, g_sub, b_glu, w_dw, b_dw, g_ln, b_ln, w_pw2, b_pw2, w_out):
    bp, tp, _ = x_prompt.shape
    bs, ts, _ = x_sample.shape
    depth = w_in.shape[0]
    n_pool = cache_k.shape[1]
    past = page_table.shape[1] * PAGE_SIZE
    pos_p = jnp.arange(tp)
    pos_s = jnp.tile(past + jnp.arange(ts), bs)
    tt_p = min(TAIL_ROWS, tp)
    seqs = min(DEC_SEQS, bs)
    mp, ms = bp * tp, bs * ts

    yp, ys = x_prompt.reshape(mp, D_MODEL), x_sample.reshape(ms, D_MODEL)
    outs = [[] for _ in range(6)]
    for l in range(depth):
        lam_init = 0.8 - 0.6 * math.exp(-0.3 * l)
        w_in_bf = w_in[l].astype(BF16)
        w_pw_bf = w_pw2[l].astype(BF16)
        w_out_bf = w_out[l].astype(BF16)
        lams = tuple(v[l].astype(F32)[None, :] for v in (lambda_q1, lambda_k1, lambda_q2, lambda_k2))
        tail_w = (w_dw[l], b_dw[l], g_ln[l], b_ln[l], w_pw_bf, b_pw2[l], w_out_bf)
        proj_w = (g_norm[l], w_in_bf, g_q[l], g_k[l], b_glu[l])

        q_p, kt_p, kbt_p, v_p, vb_p, ga_p, c_p, gc_p = _proj(yp, pos_p, *proj_w)
        q_s, kt_s, kbt_s, v_s, vb_s, ga_s, c_s, gc_s = _proj(ys, pos_s, *proj_w)

        as3 = lambda a: a.reshape(bp, tp, D_ATTN)
        cache_kt = cache_k[l].transpose(0, 2, 3, 1).reshape(n_pool, D_ATTN, PAGE_SIZE)
        cache_vr = cache_v[l].reshape(n_pool, PAGE_SIZE * ATTN_HEADS, V_DIM)
        mix_p, mix_s = _attention(as3(q_p), kbt_p, as3(vb_p), as3(ga_p),
                                  q_s.reshape(bs, ts, D_ATTN), kbt_s[0], vb_s, ga_s.astype(F32),
                                  cache_kt, cache_vr, page_table, lams, g_sub[l], lam_init)

        c3 = as3(c_p)
        blocks_per_tile = tt_p // CTX_ROWS
        prev_block = lambda bi, i: (bi, jnp.maximum(i * blocks_per_tile - 1, 0), 0)
        yp = _tail(c3, prev_block, c_p, gc_p, mix_p.reshape(mp, D_ATTN), yp, *tail_w,
                   b=bp, t=tp, bs=1, tt=tt_p, zero_first=True)
        outs[0].append(kt_p.reshape(bp, N_QK_HEADS, QK_DIM, tp).transpose(0, 3, 1, 2))
        outs[1].append(v_p.reshape(bp, tp, ATTN_HEADS, V_DIM))
        outs[2].append(c3[:, tp - CONV_CTX:])

        ctx = jnp.pad(state_conv[l].astype(F32), ((0, 0), (CTX_PAD, 0), (0, 0)))
        ys = _tail(ctx, lambda bi, i: (bi, 0, 0), c_s, gc_s, mix_s, ys, *tail_w,
                   b=bs, t=ts, bs=seqs, tt=ts, zero_first=False)
        outs[3].append(kt_s[0].T.reshape(bs, ts, N_QK_HEADS, QK_DIM))
        outs[4].append(v_s.reshape(bs, ts, ATTN_HEADS, V_DIM))
        cs3 = c_s.reshape(bs, ts, D_CONV)
        outs[5].append(jnp.concatenate([state_conv[l].astype(F32), cs3], axis=1)[:, -CONV_CTX:])

    return ((yp.reshape(bp, tp, D_MODEL), ys.reshape(bs, ts, D_MODEL))
            + tuple(jnp.stack(o) for o in outs))
```

```python
import functools
import math

import jax
import jax.numpy as jnp
from jax import lax
from jax.experimental import pallas as pl
from jax.experimental.pallas import tpu as pltpu

D_MODEL = 2048
D_ATTN = D_MODEL // 2
D_CONV = D_MODEL - D_ATTN
ATTN_HEADS = 8
N_QK_HEADS = 2 * ATTN_HEADS
QK_DIM = D_ATTN // N_QK_HEADS
V_DIM = 2 * QK_DIM
ROT_DIM = QK_DIM // 4
ROT_HALF = ROT_DIM // 2
ROPE_THETA = 500000.0
CONV_WIDTH = 31
CONV_CTX = CONV_WIDTH - 1
PAGE_SIZE = 128
SCALE = QK_DIM ** -0.5
Q_SCALE = SCALE * math.log2(math.e)
G_Q, G_K, G_V, G_GATE_A, G_GLU_A, G_GLU_B, G_GATE_C = range(7)
D_IN = 7 * D_ATTN

LANES = 128
SUBLANES = 8
MXU_DEPTH = 256
CTX_ROWS = 32
CTX_PAD = CTX_ROWS - CONV_CTX
VMEM_INTERNAL_BYTES = 12 << 20

PROJ_ROWS = 256
ATTN_ROWS = 512
ATTN_KEYS = 512
ATTN_HEADS_PER_STEP = 1
TAIL_ROWS = 256
CONV_CHUNK = 32
DEC_PAGES = 16
DEC_SEQS = 8

HEADS_PER_PASS = MXU_DEPTH // QK_DIM
N_PASSES = N_QK_HEADS // HEADS_PER_PASS

F32 = jnp.float32
BF16 = jnp.bfloat16


def _vmem_limit(pipelined_bytes, resident_bytes):
    return int(2 * pipelined_bytes + resident_bytes + VMEM_INTERNAL_BYTES)


def _nbytes(shape, dtype):
    return math.prod(shape) * jnp.dtype(dtype).itemsize


def _sigmoid(x):
    return 1.0 / (1.0 + jnp.exp(-x))


def _silu(x):
    return x * _sigmoid(x)


def _resident(shape, index_map):
    return pl.BlockSpec(shape, index_map, pipeline_mode=pl.Buffered(1))


def _proj_kernel(x_ref, gn_ref, w_ref, bd_ref, gq_ref, gkt_ref, cos_ref, s1_ref, s2_ref,
                 cost_ref, sint_ref, bglu_ref,
                 q_ref, kt_ref, kbt_ref, v_ref, vb_ref, ga_ref, c_ref, gc_ref, h_scr, zk_scr):
    x = x_ref[...]
    ms = jnp.mean(x * x, axis=-1, keepdims=True)
    h_scr[...] = (x * lax.rsqrt(ms + 1e-6) * gn_ref[...]).astype(BF16)

    def group(g):
        return jnp.dot(h_scr[...], w_ref[:, g * D_ATTN:(g + 1) * D_ATTN],
                       preferred_element_type=F32)

    zq = group(G_Q)
    for c in range(D_ATTN // LANES):
        zc = zq[:, c * LANES:(c + 1) * LANES]
        ss = jnp.dot((zc * zc).astype(BF16), bd_ref[...], preferred_element_type=F32)
        y = zc * lax.rsqrt(ss * (1.0 / QK_DIM) + 1e-6) * gq_ref[...]
        y = (y * cos_ref[...]
             + pltpu.roll(y, LANES - ROT_HALF, 1) * s1_ref[...]
             + pltpu.roll(y, ROT_HALF, 1) * s2_ref[...])
        q_ref[:, c * LANES:(c + 1) * LANES] = (y * Q_SCALE).astype(BF16)

    zk_scr[...] = group(G_K)
    zkt = zk_scr[...].T
    for hd in range(N_QK_HEADS):
        blk = zkt[hd * QK_DIM:(hd + 1) * QK_DIM, :]
        ss = jnp.sum(blk * blk, axis=0, keepdims=True)
        y = blk * lax.rsqrt(ss * (1.0 / QK_DIM) + 1e-6) * gkt_ref[...]
        x1, x2 = y[0:ROT_HALF], y[ROT_HALF:ROT_DIM]
        cos, sin = cost_ref[...], sint_ref[...]
        y = jnp.concatenate([x1 * cos - x2 * sin, x2 * cos + x1 * sin, y[ROT_DIM:]], axis=0)
        kt_ref[0, hd * QK_DIM:(hd + 1) * QK_DIM, :] = y
        kbt_ref[0, hd * QK_DIM:(hd + 1) * QK_DIM, :] = y.astype(BF16)

    zv = group(G_V)
    v_ref[...] = zv
    vb_ref[...] = zv.astype(BF16)
    ga_ref[...] = _silu(group(G_GATE_A)).astype(BF16)
    u = group(G_GLU_A) + bglu_ref[:, :D_CONV]
    c_ref[...] = u * _sigmoid(group(G_GLU_B) + bglu_ref[:, D_CONV:])
    gc_ref[...] = _silu(group(G_GATE_C)).astype(BF16)


def _rope_tables(pos):
    inv = jnp.power(ROPE_THETA, -jnp.arange(ROT_HALF, dtype=F32) * (2.0 / ROT_DIM))
    ang = pos.astype(F32)[:, None] * inv[None, :]
    cos, sin = jnp.cos(ang), jnp.sin(ang)
    n = pos.shape[0]
    rest = QK_DIM - ROT_DIM
    zeros = jnp.zeros((n, ROT_HALF), F32)
    cos_t = jnp.concatenate([cos, cos, jnp.ones((n, rest), F32)], axis=1)
    s1_t = jnp.concatenate([-sin, zeros, jnp.zeros((n, rest), F32)], axis=1)
    s2_t = jnp.concatenate([zeros, sin, jnp.zeros((n, rest), F32)], axis=1)
    rep = LANES // QK_DIM
    rows = tuple(jnp.tile(t, (1, rep)) for t in (cos_t, s1_t, s2_t))
    return rows + (cos.T, sin.T)


def _proj(x2d, pos, g_norm, w_in_bf, g_q, g_k, b_glu):
    m = x2d.shape[0]
    p = pos.shape[0]
    tm = min(PROJ_ROWS, m)
    assert m % p == 0 and p % tm == 0
    n_pos_blocks = p // tm
    cos_t, s1_t, s2_t, cos_tt, sin_tt = _rope_tables(pos)
    head = jnp.arange(LANES) // QK_DIM
    bd = (head[:, None] == head[None, :]).astype(BF16)
    gq = jnp.tile(g_q.astype(F32), LANES // QK_DIM)[None, :]
    gkt = jnp.broadcast_to(g_k.astype(F32)[:, None], (QK_DIM, tm))

    row = lambda i: (i, 0)
    const = lambda i: (0, 0)
    tab = lambda i: (i % n_pos_blocks, 0)
    tab_t = lambda i: (0, i % n_pos_blocks)
    kt_map = lambda i: (i // n_pos_blocks, 0, i % n_pos_blocks)
    in_specs = [
        pl.BlockSpec((tm, D_MODEL), row),
        _resident((1, D_MODEL), const),
        _resident((D_MODEL, D_IN), const),
        _resident((LANES, LANES), const),
        _resident((1, LANES), const),
        _resident((QK_DIM, tm), const),
        pl.BlockSpec((tm, LANES), tab),
        pl.BlockSpec((tm, LANES), tab),
        pl.BlockSpec((tm, LANES), tab),
        pl.BlockSpec((ROT_HALF, tm), tab_t),
        pl.BlockSpec((ROT_HALF, tm), tab_t),
        _resident((1, 2 * D_CONV), const),
    ]
    rows_out = lambda dt: (jax.ShapeDtypeStruct((m, D_ATTN), dt), pl.BlockSpec((tm, D_ATTN), row))
    kt_out = lambda dt: (jax.ShapeDtypeStruct((m // p, D_ATTN, p), dt),
                         pl.BlockSpec((1, D_ATTN, tm), kt_map))
    outs = [rows_out(BF16), kt_out(F32), kt_out(BF16), rows_out(F32), rows_out(BF16),
            rows_out(BF16), rows_out(F32), rows_out(BF16)]
    pipelined = (_nbytes((tm, D_MODEL), F32) + 3 * _nbytes((tm, LANES), F32)
                 + sum(_nbytes((tm, D_ATTN), o[0].dtype) for o in outs))
    resident = (_nbytes((D_MODEL, D_IN), BF16) + _nbytes((tm, D_MODEL), BF16)
                + 4 * _nbytes((tm, D_ATTN), F32))
    return pl.pallas_call(
        _proj_kernel,
        grid=(m // tm,),
        in_specs=in_specs,
        out_specs=tuple(o[1] for o in outs),
        out_shape=tuple(o[0] for o in outs),
        scratch_shapes=[pltpu.VMEM((tm, D_MODEL), BF16), pltpu.VMEM((tm, D_ATTN), F32)],
        compiler_params=pltpu.CompilerParams(
            dimension_semantics=("arbitrary",),
            vmem_limit_bytes=_vmem_limit(pipelined, resident)),
        name="proj",
    )(x2d, g_norm.astype(F32)[None, :], w_in_bf, bd, gq, gkt, cos_t, s1_t, s2_t,
      cos_tt, sin_tt, b_glu.astype(F32)[None, :])


def _lambda_full(lq1_ref, lk1_ref, lq2_ref, lk2_ref, lam_init):
    a = jnp.sum(lq1_ref[...] * lk1_ref[...], axis=-1, keepdims=True)
    b = jnp.sum(lq2_ref[...] * lk2_ref[...], axis=-1, keepdims=True)
    return jnp.exp(a) - jnp.exp(b) + lam_init


def _subnorm_gate(o, gsub, gate, lam_init):
    ms = jnp.mean(o * o, axis=-1, keepdims=True)
    y = o * lax.rsqrt(ms + 1e-5) * gsub
    return y * (1.0 - lam_init) * gate


def _prompt_tile(qi, lam, gsub_ref, q_ref, kt_ref, v_ref, ga_ref, o_ref, m_scr, l_scr, acc_scr, s_scr,
                 *, tq, tk, hp, lam_init):
    lane = lax.broadcasted_iota(jnp.int32, (tq, LANES), 1)
    chains = []
    for h in range(hp):
        qh = q_ref[0, :, h * LANES:(h + 1) * LANES]
        zero = jnp.zeros_like(qh)
        chains.append((h, jnp.where(lane < QK_DIM, qh, zero)))
        chains.append((h, jnp.where(lane >= QK_DIM, qh, zero)))
    m_scr[...] = jnp.full(m_scr.shape, -jnp.inf, F32)
    l_scr[...] = jnp.zeros(l_scr.shape, F32)
    acc_scr[...] = jnp.zeros(acc_scr.shape, F32)
    n_lane_tiles = tk // LANES

    def scores(c, j):
        h, qc = chains[c]
        start = pl.multiple_of(j * tk, tk)
        kt = kt_ref[0, h * LANES:(h + 1) * LANES, pl.ds(start, tk)]
        return jnp.dot(qc, kt, preferred_element_type=F32)

    def step(j, masked, prefetch):
        start = pl.multiple_of(j * tk, tk)
        if masked:
            row = qi * tq + lax.broadcasted_iota(jnp.int32, (tq, tk), 0)
            col = start + lax.broadcasted_iota(jnp.int32, (tq, tk), 1)
            visible = col <= row
        for c, (h, _) in enumerate(chains):
            vt = v_ref[0, pl.ds(start, tk), h * LANES:(h + 1) * LANES]
            s = s_scr[c]
            if prefetch:
                s_scr[c] = scores(c, j + 1)
            if masked:
                s = jnp.where(visible, s, -jnp.inf)
            tiles = [s[:, t * LANES:(t + 1) * LANES] for t in range(n_lane_tiles)]
            tile_max = functools.reduce(jnp.maximum, tiles)
            m_old = m_scr[c]
            m_new = jnp.maximum(m_old, jnp.max(tile_max, axis=1, keepdims=True))
            alpha = jnp.exp2(m_old - m_new)
            ps = [jnp.exp2(t - m_new) for t in tiles]
            l_scr[c] = alpha * l_scr[c] + functools.reduce(jnp.add, ps)
            pb = jnp.concatenate([p.astype(BF16) for p in ps], axis=1)
            acc_scr[c] = alpha * acc_scr[c] + jnp.dot(pb, vt, preferred_element_type=F32)
            m_scr[c] = m_new

    def body(j, carry):
        step(j, False, True)
        return carry

    tiles_per_q = tq // tk
    for c in range(len(chains)):
        s_scr[c] = scores(c, 0)
    lax.fori_loop(0, qi * tiles_per_q, body, 0)
    for jj in range(tiles_per_q):
        step(qi * tiles_per_q + jj, True, jj + 1 < tiles_per_q)

    for h in range(hp):
        o1 = acc_scr[2 * h] / jnp.sum(l_scr[2 * h], axis=1, keepdims=True)
        o2 = acc_scr[2 * h + 1] / jnp.sum(l_scr[2 * h + 1], axis=1, keepdims=True)
        gate = ga_ref[0, :, h * LANES:(h + 1) * LANES].astype(F32)
        y = _subnorm_gate(o1 - lam * o2, gsub_ref[...], gate, lam_init)
        o_ref[0, :, h * LANES:(h + 1) * LANES] = y.astype(BF16)


def _decode_slice(seq, s_idx, n_steps, lam, gsub_ref, qc_ref, ktn_ref, vn_ref, ga_ref, k_refs, v_refs,
                  o_ref, m_scr, l_scr, acc_scr, *, n_q, lam_init):
    n_pages_step = len(k_refs)
    rows_pass = HEADS_PER_PASS * n_q
    rows_h = 2 * n_q

    @pl.when(s_idx == 0)
    def _():
        m_scr[...] = jnp.full(m_scr.shape, -jnp.inf, F32)
        l_scr[...] = jnp.zeros(l_scr.shape, F32)
        acc_scr[...] = jnp.zeros(acc_scr.shape, F32)

    def scores(kt):
        return jnp.concatenate(
            [jnp.dot(qc_ref[0, c * rows_pass:(c + 1) * rows_pass, :],
                     kt[c * MXU_DEPTH:(c + 1) * MXU_DEPTH, :], preferred_element_type=F32)
             for c in range(N_PASSES)], axis=0)

    def update(s, values_of_head, l_lanes):
        m_old = m_scr[...]
        m_new = jnp.maximum(m_old, jnp.max(s, axis=1, keepdims=True))
        alpha = jnp.exp2(m_old - m_new)
        p = jnp.exp2(s - m_new)
        p_sum = p[:, 0:l_lanes]
        for t in range(1, s.shape[1] // l_lanes):
            p_sum = p_sum + p[:, t * l_lanes:(t + 1) * l_lanes]
        if l_lanes < LANES:
            p_sum = jnp.concatenate([p_sum, jnp.zeros((s.shape[0], LANES - l_lanes), F32)], axis=1)
        l_scr[...] = alpha * l_scr[...] + p_sum
        pb = p.astype(BF16)
        blocks = []
        for h0 in range(0, ATTN_HEADS, 2):
            w = jnp.concatenate([values_of_head(h0), values_of_head(h0 + 1)], axis=1)
            r = jnp.dot(pb[h0 * rows_h:(h0 + 2) * rows_h, :], w, preferred_element_type=F32)
            blocks += [r[:rows_h, :V_DIM], r[rows_h:, V_DIM:]]
        pv = jnp.concatenate(blocks, axis=0)
        acc_scr[...] = alpha * acc_scr[...] + pv
        m_scr[...] = m_new

    s_pages = scores(jnp.concatenate([k_refs[g][0].astype(BF16) for g in range(n_pages_step)],
                                     axis=1))

    def cached_values(h):
        return jnp.concatenate(
            [v_refs[g][0, pl.ds(h, PAGE_SIZE, stride=ATTN_HEADS), :].astype(BF16)
             for g in range(n_pages_step)], axis=0)

    update(s_pages, cached_values, LANES)

    @pl.when(s_idx == n_steps - 1)
    def _():
        s = scores(ktn_ref[...])
        n_new = s.shape[1]
        tok = lax.broadcasted_iota(jnp.int32, s.shape, 1)
        qry = lax.broadcasted_iota(jnp.int32, s.shape, 0) % n_q
        lo = seq * n_q
        visible = (tok >= lo) & (tok <= lo + qry)
        update(jnp.where(visible, s, -jnp.inf),
               lambda h: vn_ref[:, h * V_DIM:(h + 1) * V_DIM], min(n_new, LANES))

        o_all = acc_scr[...] / jnp.sum(l_scr[...], axis=1, keepdims=True)
        for h in range(ATTN_HEADS):
            o = (o_all[h * rows_h:h * rows_h + n_q]
                 - lam * o_all[h * rows_h + n_q:(h + 1) * rows_h])
            gate = ga_ref[:, h * V_DIM:(h + 1) * V_DIM]
            o_ref[:, h * V_DIM:(h + 1) * V_DIM] = _subnorm_gate(o, gsub_ref[...], gate, lam_init)


def _attn_kernel(pt_ref, lq1_ref, lk1_ref, lq2_ref, lk2_ref, gsub_ref,
                 q_ref, kt_ref, v_ref, gap_ref, qc_ref, ktn_ref, vn_ref, gas_ref, *refs,
                 n_pages_step, n_prompt_steps, n_decode_steps, nq, steps_per_seq,
                 tq, tk, hp, n_q, lam_init):
    del pt_ref
    k_refs = refs[:n_pages_step]
    v_refs = refs[n_pages_step:2 * n_pages_step]
    op_ref, od_ref, pm_scr, pl_scr, pacc_scr, ps_scr, dm_scr, dl_scr, dacc_scr = refs[2 * n_pages_step:]
    step = pl.program_id(0)
    lam = _lambda_full(lq1_ref, lk1_ref, lq2_ref, lk2_ref, lam_init)

    def prompt():
        _prompt_tile(step % nq, lam, gsub_ref, q_ref, kt_ref, v_ref, gap_ref, op_ref,
                     pm_scr, pl_scr, pacc_scr, ps_scr, tq=tq, tk=tk, hp=hp, lam_init=lam_init)

    def decode():
        _decode_slice(step // steps_per_seq, step % steps_per_seq, steps_per_seq, lam, gsub_ref,
                      qc_ref, ktn_ref, vn_ref, gas_ref, k_refs, v_refs, od_ref,
                      dm_scr, dl_scr, dacc_scr, n_q=n_q, lam_init=lam_init)

    n_steps = max(n_prompt_steps, n_decode_steps)
    prompt() if n_prompt_steps == n_steps else pl.when(step < n_prompt_steps)(prompt)
    decode() if n_decode_steps == n_steps else pl.when(step < n_decode_steps)(decode)


def _attention(q_p, kbt_p, vb_p, ga_p, q_s, ktn, vn, ga_s, cache_kt, cache_v, page_table,
               lams, g_sub, lam_init):
    b, t, _ = q_p.shape
    tq = min(ATTN_ROWS, t)
    tk = min(ATTN_KEYS, tq)
    hp = ATTN_HEADS_PER_STEP
    assert t % tq == 0 and tq % tk == 0 and ATTN_HEADS % hp == 0
    nq, nh = t // tq, ATTN_HEADS // hp
    n_prompt_steps = b * nh * nq
    width = hp * LANES

    bd, s, _ = q_s.shape
    n_pages = page_table.shape[1]
    g = min(DEC_PAGES, n_pages)
    assert n_pages % g == 0 and s == SUBLANES
    steps_per_seq = n_pages // g
    n_decode_steps = bd * steps_per_seq
    n_rows = N_QK_HEADS * s
    q6 = q_s.reshape(bd, s, N_PASSES, HEADS_PER_PASS, QK_DIM)
    eye = jnp.eye(HEADS_PER_PASS, dtype=q_s.dtype)
    qc = jnp.einsum('bqchd,hg->bchqgd', q6, eye).reshape(bd, n_rows, MXU_DEPTH)

    def prompt_idx(i):
        i = jnp.minimum(i, n_prompt_steps - 1)
        return i // (nh * nq), (i // nq) % nh, i % nq

    def decode_idx(i):
        i = jnp.minimum(i, n_decode_steps - 1)
        return i // steps_per_seq, i % steps_per_seq

    def p_tile(i, pt):
        bi, h, qi = prompt_idx(i)
        return bi, qi, h

    def p_keys(i, pt):
        bi, h, _ = prompt_idx(i)
        return bi, h, 0

    def p_vals(i, pt):
        bi, h, _ = prompt_idx(i)
        return bi, 0, h

    def page_spec(gi):
        def idx(i, pt):
            seq, grp = decode_idx(i)
            return pt[seq, grp * g + gi], 0, 0
        return pl.BlockSpec((1, D_ATTN, PAGE_SIZE), idx)

    const2 = lambda shape: pl.BlockSpec(shape, lambda i, pt: (0, 0))
    small = const2((1, QK_DIM))
    tile = pl.BlockSpec((1, tq, width), p_tile)
    per_seq = pl.BlockSpec((s, D_ATTN), lambda i, pt: (decode_idx(i)[0], 0))
    in_specs = ([small, small, small, small, const2((1, V_DIM)),
                 tile, pl.BlockSpec((1, width, t), p_keys), pl.BlockSpec((1, t, width), p_vals), tile,
                 pl.BlockSpec((1, n_rows, MXU_DEPTH), lambda i, pt: (decode_idx(i)[0], 0, 0)),
                 const2((D_ATTN, bd * s)), const2((bd * s, D_ATTN)), per_seq]
                + [page_spec(gi) for gi in range(g)] + [page_spec(gi) for gi in range(g)])
    p_state = (2 * hp, tq, LANES)
    p_scores = (2 * hp, tq, tk)
    pipelined = (3 * _nbytes((tq, width), BF16) + 2 * _nbytes((t, width), BF16)
                 + 2 * g * _nbytes((PAGE_SIZE, D_ATTN), F32) + _nbytes((n_rows, MXU_DEPTH), BF16)
                 + 2 * _nbytes((bd * s, D_ATTN), BF16) + 2 * _nbytes((s, D_ATTN), F32))
    resident = (3 * _nbytes(p_state, F32) + 3 * _nbytes(p_scores, F32)
                + 2 * _nbytes((g * PAGE_SIZE, D_ATTN), BF16) + 3 * _nbytes((n_rows, LANES), F32)
                + 4 * _nbytes((n_rows, g * PAGE_SIZE), F32))
    grid_spec = pltpu.PrefetchScalarGridSpec(
        num_scalar_prefetch=1,
        grid=(max(n_prompt_steps, n_decode_steps),),
        in_specs=in_specs,
        out_specs=(tile, per_seq),
        scratch_shapes=[pltpu.VMEM(p_state, F32), pltpu.VMEM(p_state, F32), pltpu.VMEM(p_state, F32),
                        pltpu.VMEM(p_scores, F32),
                        pltpu.VMEM((n_rows, 1), F32), pltpu.VMEM((n_rows, LANES), F32),
                        pltpu.VMEM((n_rows, V_DIM), F32)],
    )
    return pl.pallas_call(
        functools.partial(_attn_kernel, n_pages_step=g, n_prompt_steps=n_prompt_steps,
                          n_decode_steps=n_decode_steps, nq=nq, steps_per_seq=steps_per_seq,
                          tq=tq, tk=tk, hp=hp, n_q=s, lam_init=lam_init),
        grid_spec=grid_spec,
        out_shape=(jax.ShapeDtypeStruct((b, t, D_ATTN), BF16),
                   jax.ShapeDtypeStruct((bd * s, D_ATTN), F32)),
        compiler_params=pltpu.CompilerParams(
            dimension_semantics=("arbitrary",),
            vmem_limit_bytes=_vmem_limit(pipelined, resident)),
        name="attention",
    )(page_table, *lams, g_sub.astype(F32)[None, :], q_p, kbt_p, vb_p, ga_p, qc, ktn, vn, ga_s,
      *([cache_kt] * g), *([cache_v] * g))


def _tail_kernel(ctx_ref, c_ref, gc_ref, ma_ref, x_ref, wdw_ref, bdw_ref, gln_ref, bln_ref,
                 wpw_ref, bpw_ref, wout_ref, y_ref, ext_scr, sh_scr, cv_scr, *, bs, tt, rc, zero_first):
    i = pl.program_id(1)
    sh_rows = tt + CTX_ROWS - SUBLANES
    for b in range(bs):
        ctx = ctx_ref[b]
        if zero_first:
            ctx = jnp.where(i == 0, jnp.zeros_like(ctx), ctx)
        ext_scr[0:CTX_ROWS, :] = ctx
        ext_scr[CTX_ROWS:CTX_ROWS + tt, :] = c_ref[b * tt:(b + 1) * tt, :]
        for s in range(1, SUBLANES):
            sh_scr[s - 1, 0:sh_rows, :] = ext_scr[s:s + sh_rows, :]

        def chunk(ci, carry, b=b):
            r0 = pl.multiple_of(ci * rc, rc)
            acc = jnp.broadcast_to(bdw_ref[...], (rc, D_CONV))
            for tap in range(CONV_WIDTH):
                whole, s = divmod(CTX_PAD + tap, SUBLANES)
                rows = pl.ds(r0 + whole * SUBLANES, rc)
                win = ext_scr[rows, :] if s == 0 else sh_scr[s - 1, rows, :]
                acc = acc + win * jnp.tile(wdw_ref[tap], (rc // SUBLANES, 1))
            cv_scr[pl.ds(b * tt + r0, rc), :] = acc
            return carry

        lax.fori_loop(0, tt // rc, chunk, 0)

    cv = cv_scr[...]
    mu = jnp.mean(cv, axis=-1, keepdims=True)
    d = cv - mu
    var = jnp.mean(d * d, axis=-1, keepdims=True)
    yn = d * lax.rsqrt(var + 1e-5) * gln_ref[...] + bln_ref[...]
    act = _silu(yn).astype(BF16)
    pw = jnp.dot(act, wpw_ref[...], preferred_element_type=F32) + bpw_ref[...]
    mix_c = (pw * gc_ref[...].astype(F32)).astype(BF16)
    mix_a = ma_ref[...].astype(BF16)
    y_ref[...] = (x_ref[...]
                  + jnp.dot(mix_a, wout_ref[0:D_ATTN, :], preferred_element_type=F32)
                  + jnp.dot(mix_c, wout_ref[D_ATTN:, :], preferred_element_type=F32))


def _tail(ctx, ctx_map, c, gc, mix_a, x, w_dw, b_dw, g_ln, b_ln, w_pw_bf, b_pw2, w_out_bf,
          *, b, t, bs, tt, zero_first):
    assert b % bs == 0 and t % tt == 0 and (bs == 1 or tt == t)
    rc = min(CONV_CHUNK, tt)
    assert tt % rc == 0
    tiles_per_seq = t // tt
    tile = lambda width: pl.BlockSpec((bs * tt, width), lambda bi, i: (bi * tiles_per_seq + i, 0))
    const = lambda shape: _resident(shape, lambda bi, i: (0,) * len(shape))
    in_specs = [pl.BlockSpec((bs, CTX_ROWS, D_CONV), ctx_map),
                tile(D_CONV), tile(D_CONV), tile(D_ATTN), tile(D_MODEL),
                const((CONV_WIDTH, SUBLANES, D_CONV)), const((1, D_CONV)), const((1, D_CONV)),
                const((1, D_CONV)), const((D_CONV, D_CONV)), const((1, D_CONV)),
                const((D_MODEL, D_MODEL))]
    rows = bs * tt
    ext_shape = (CTX_ROWS + tt, D_CONV)
    sh_shape = (SUBLANES - 1,) + ext_shape
    pipelined = (_nbytes((bs, CTX_ROWS, D_CONV), F32) + _nbytes((rows, D_CONV), F32)
                 + _nbytes((rows, D_CONV), gc.dtype) + _nbytes((rows, D_ATTN), mix_a.dtype)
                 + 2 * _nbytes((rows, D_MODEL), F32))
    resident = (_nbytes((CONV_WIDTH, SUBLANES, D_CONV), F32) + _nbytes((D_CONV, D_CONV), BF16)
                + _nbytes((D_MODEL, D_MODEL), BF16) + _nbytes(ext_shape, F32)
                + _nbytes(sh_shape, F32) + 4 * _nbytes((rows, D_CONV), F32))
    w_taps = jnp.broadcast_to(w_dw.astype(F32)[:, None, :], (CONV_WIDTH, SUBLANES, D_CONV))
    return pl.pallas_call(
        functools.partial(_tail_kernel, bs=bs, tt=tt, rc=rc, zero_first=zero_first),
        grid=(b // bs, tiles_per_seq),
        in_specs=in_specs,
        out_specs=tile(D_MODEL),
        out_shape=jax.ShapeDtypeStruct((b * t, D_MODEL), F32),
        scratch_shapes=[pltpu.VMEM(ext_shape, F32), pltpu.VMEM(sh_shape, F32),
                        pltpu.VMEM((rows, D_CONV), F32)],
        compiler_params=pltpu.CompilerParams(
            dimension_semantics=("arbitrary", "arbitrary"),
            vmem_limit_bytes=_vmem_limit(pipelined, resident)),
        name="tail",
    )(ctx, c, gc, mix_a, x, w_taps, b_dw.astype(F32)[None, :], g_ln.astype(F32)[None, :],
      b_ln.astype(F32)[None, :], w_pw_bf, b_pw2.astype(F32)[None, :], w_out_bf)


def kernel(x_prompt, x_sample, cache_k, cache_v, state_conv, page_table, g_norm, w_in, g_q, g_k, lambda_q1, lambda_k1, lambda_q2, lambda_k2, g_sub, b_glu, w_dw, b_dw, g_ln, b_ln, w_pw2, b_pw2, w_out):
    bp, tp, _ = x_prompt.shape
    bs, ts, _ = x_sample.shape
    depth = w_in.shape[0]
    n_pool = cache_k.shape[1]
    past = page_table.shape[1] * PAGE_SIZE
    pos_p = jnp.arange(tp)
    pos_s = jnp.tile(past + jnp.arange(ts), bs)
    tt_p = min(TAIL_ROWS, tp)
    seqs = min(DEC_SEQS, bs)
    mp, ms = bp * tp, bs * ts

    yp, ys = x_prompt.reshape(mp, D_MODEL), x_sample.reshape(ms, D_MODEL)
    outs = [[] for _ in range(6)]
    for l in range(depth):
        lam_init = 0.8 - 0.6 * math.exp(-0.3 * l)
        w_in_bf = w_in[l].astype(BF16)
        w_pw_bf = w_pw2[l].astype(BF16)
        w_out_bf = w_out[l].astype(BF16)
        lams = tuple(v[l].astype(F32)[None, :] for v in (lambda_q1, lambda_k1, lambda_q2, lambda_k2))
        tail_w = (w_dw[l], b_dw[l], g_ln[l], b_ln[l], w_pw_bf, b_pw2[l], w_out_bf)
        proj_w = (g_norm[l], w_in_bf, g_q[l], g_k[l], b_glu[l])

        q_p, kt_p, kbt_p, v_p, vb_p, ga_p, c_p, gc_p = _proj(yp, pos_p, *proj_w)
        q_s, kt_s, kbt_s, v_s, vb_s, ga_s, c_s, gc_s = _proj(ys, pos_s, *proj_w)

        as3 = lambda a: a.reshape(bp, tp, D_ATTN)
        cache_kt = cache_k[l].transpose(0, 2, 3, 1).reshape(n_pool, D_ATTN, PAGE_SIZE)
        cache_vr = cache_v[l].reshape(n_pool, PAGE_SIZE * ATTN_HEADS, V_DIM)
        mix_p, mix_s = _attention(as3(q_p), kbt_p, as3(vb_p), as3(ga_p),
                                  q_s.reshape(bs, ts, D_ATTN), kbt_s[0], vb_s, ga_s.astype(F32),
                                  cache_kt, cache_vr, page_table, lams, g_sub[l], lam_init)

        c3 = as3(c_p)
        blocks_per_tile = tt_p // CTX_ROWS
        prev_block = lambda bi, i: (bi, jnp.maximum(i * blocks_per_tile - 1, 0), 0)
        yp = _tail(c3, prev_block, c_p, gc_p, mix_p.reshape(mp, D_ATTN), yp, *tail_w,
                   b=bp, t=tp, bs=1, tt=tt_p, zero_first=True)
        outs[0].append(kt_p.reshape(bp, N_QK_HEADS, QK_DIM, tp).transpose(0, 3, 1, 2))
        outs[1].append(v_p.reshape(bp, tp, ATTN_HEADS, V_DIM))
        outs[2].append(c3[:, tp - CONV_CTX:])

        ctx = jnp.pad(state_conv[l].astype(F32), ((0, 0), (CTX_PAD, 0), (0, 0)))
        ys = _tail(ctx, lambda bi, i: (bi, 0, 0), c_s, gc_s, mix_s, ys, *tail_w,
                   b=bs, t=ts, bs=seqs, tt=ts, zero_first=False)
        outs[3].append(kt_s[0].T.reshape(bs, ts, N_QK_HEADS, QK_DIM))
        outs[4].append(v_s.reshape(bs, ts, ATTN_HEADS, V_DIM))
        cs3 = c_s.reshape(bs, ts, D_CONV)
        outs[5].append(jnp.concatenate([state_conv[l].astype(F32), cs3], axis=1)[:, -CONV_CTX:])

    return ((yp.reshape(bp, tp, D_MODEL), ys.reshape(bs, ts, D_MODEL))
            + tuple(jnp.stack(o) for o in outs))
```

```python
import functools
import math

import jax
import jax.numpy as jnp
from jax import lax
from jax.experimental import pallas as pl
from jax.experimental.pallas import tpu as pltpu

D_MODEL = 2048
D_ATTN = D_MODEL // 2
D_CONV = D_MODEL - D_ATTN
ATTN_HEADS = 8
N_QK_HEADS = 2 * ATTN_HEADS
QK_DIM = D_ATTN // N_QK_HEADS
V_DIM = 2 * QK_DIM
ROT_DIM = QK_DIM // 4
ROT_HALF = ROT_DIM // 2
ROPE_THETA = 500000.0
CONV_WIDTH = 31
CONV_CTX = CONV_WIDTH - 1
PAGE_SIZE = 128
SCALE = QK_DIM ** -0.5
Q_SCALE = SCALE * math.log2(math.e)
G_Q, G_K, G_V, G_GATE_A, G_GLU_A, G_GLU_B, G_GATE_C = range(7)
D_IN = 7 * D_ATTN

LANES = 128
SUBLANES = 8
MXU_DEPTH = 256
CTX_ROWS = 32
CTX_PAD = CTX_ROWS - CONV_CTX
VMEM_INTERNAL_BYTES = 12 << 20

PROJ_ROWS = 256
ATTN_ROWS = 512
ATTN_KEYS = 512
ATTN_HEADS_PER_STEP = 1
TAIL_ROWS = 256
CONV_CHUNK = 32
DEC_PAGES = 16
DEC_PAGE_CYCLE = (12, 14, 18, 20)
DEC_SEQS = 8

HEADS_PER_PASS = MXU_DEPTH // QK_DIM
N_PASSES = N_QK_HEADS // HEADS_PER_PASS

F32 = jnp.float32
BF16 = jnp.bfloat16


def _vmem_limit(pipelined_bytes, resident_bytes):
    return int(2 * pipelined_bytes + resident_bytes + VMEM_INTERNAL_BYTES)


def _nbytes(shape, dtype):
    return math.prod(shape) * jnp.dtype(dtype).itemsize


def _sigmoid(x):
    return 1.0 / (1.0 + jnp.exp(-x))


def _silu(x):
    return x * _sigmoid(x)


def _resident(shape, index_map):
    return pl.BlockSpec(shape, index_map, pipeline_mode=pl.Buffered(1))


def _proj_kernel(x_ref, gn_ref, w_ref, bd_ref, gq_ref, gkt_ref, cos_ref, s1_ref, s2_ref,
                 cost_ref, sint_ref, bglu_ref,
                 q_ref, kt_ref, kbt_ref, v_ref, vb_ref, ga_ref, c_ref, gc_ref, h_scr, zk_scr):
    x = x_ref[...]
    ms = jnp.mean(x * x, axis=-1, keepdims=True)
    h_scr[...] = (x * lax.rsqrt(ms + 1e-6) * gn_ref[...]).astype(BF16)

    def group(g):
        return jnp.dot(h_scr[...], w_ref[:, g * D_ATTN:(g + 1) * D_ATTN],
                       preferred_element_type=F32)

    zq = group(G_Q)
    for c in range(D_ATTN // LANES):
        zc = zq[:, c * LANES:(c + 1) * LANES]
        ss = jnp.dot((zc * zc).astype(BF16), bd_ref[...], preferred_element_type=F32)
        y = zc * lax.rsqrt(ss * (1.0 / QK_DIM) + 1e-6) * gq_ref[...]
        y = (y * cos_ref[...]
             + pltpu.roll(y, LANES - ROT_HALF, 1) * s1_ref[...]
             + pltpu.roll(y, ROT_HALF, 1) * s2_ref[...])
        q_ref[:, c * LANES:(c + 1) * LANES] = (y * Q_SCALE).astype(BF16)

    zk_scr[...] = group(G_K)
    zkt = zk_scr[...].T
    for hd in range(N_QK_HEADS):
        blk = zkt[hd * QK_DIM:(hd + 1) * QK_DIM, :]
        ss = jnp.sum(blk * blk, axis=0, keepdims=True)
        y = blk * lax.rsqrt(ss * (1.0 / QK_DIM) + 1e-6) * gkt_ref[...]
        x1, x2 = y[0:ROT_HALF], y[ROT_HALF:ROT_DIM]
        cos, sin = cost_ref[...], sint_ref[...]
        y = jnp.concatenate([x1 * cos - x2 * sin, x2 * cos + x1 * sin, y[ROT_DIM:]], axis=0)
        kt_ref[0, hd * QK_DIM:(hd + 1) * QK_DIM, :] = y
        kbt_ref[0, hd * QK_DIM:(hd + 1) * QK_DIM, :] = y.astype(BF16)

    zv = group(G_V)
    v_ref[...] = zv
    vb_ref[...] = zv.astype(BF16)
    ga_ref[...] = _silu(group(G_GATE_A)).astype(BF16)
    u = group(G_GLU_A) + bglu_ref[:, :D_CONV]
    c_ref[...] = u * _sigmoid(group(G_GLU_B) + bglu_ref[:, D_CONV:])
    gc_ref[...] = _silu(group(G_GATE_C)).astype(BF16)


def _rope_tables(pos):
    inv = jnp.power(ROPE_THETA, -jnp.arange(ROT_HALF, dtype=F32) * (2.0 / ROT_DIM))
    ang = pos.astype(F32)[:, None] * inv[None, :]
    cos, sin = jnp.cos(ang), jnp.sin(ang)
    n = pos.shape[0]
    rest = QK_DIM - ROT_DIM
    zeros = jnp.zeros((n, ROT_HALF), F32)
    cos_t = jnp.concatenate([cos, cos, jnp.ones((n, rest), F32)], axis=1)
    s1_t = jnp.concatenate([-sin, zeros, jnp.zeros((n, rest), F32)], axis=1)
    s2_t = jnp.concatenate([zeros, sin, jnp.zeros((n, rest), F32)], axis=1)
    rep = LANES // QK_DIM
    rows = tuple(jnp.tile(t, (1, rep)) for t in (cos_t, s1_t, s2_t))
    return rows + (cos.T, sin.T)


def _proj(x2d, pos, g_norm, w_in_bf, g_q, g_k, b_glu):
    m = x2d.shape[0]
    p = pos.shape[0]
    tm = min(PROJ_ROWS, m)
    assert m % p == 0 and p % tm == 0
    n_pos_blocks = p // tm
    cos_t, s1_t, s2_t, cos_tt, sin_tt = _rope_tables(pos)
    head = jnp.arange(LANES) // QK_DIM
    bd = (head[:, None] == head[None, :]).astype(BF16)
    gq = jnp.tile(g_q.astype(F32), LANES // QK_DIM)[None, :]
    gkt = jnp.broadcast_to(g_k.astype(F32)[:, None], (QK_DIM, tm))

    row = lambda i: (i, 0)
    const = lambda i: (0, 0)
    tab = lambda i: (i % n_pos_blocks, 0)
    tab_t = lambda i: (0, i % n_pos_blocks)
    kt_map = lambda i: (i // n_pos_blocks, 0, i % n_pos_blocks)
    in_specs = [
        pl.BlockSpec((tm, D_MODEL), row),
        _resident((1, D_MODEL), const),
        _resident((D_MODEL, D_IN), const),
        _resident((LANES, LANES), const),
        _resident((1, LANES), const),
        _resident((QK_DIM, tm), const),
        pl.BlockSpec((tm, LANES), tab),
        pl.BlockSpec((tm, LANES), tab),
        pl.BlockSpec((tm, LANES), tab),
        pl.BlockSpec((ROT_HALF, tm), tab_t),
        pl.BlockSpec((ROT_HALF, tm), tab_t),
        _resident((1, 2 * D_CONV), const),
    ]
    rows_out = lambda dt: (jax.ShapeDtypeStruct((m, D_ATTN), dt), pl.BlockSpec((tm, D_ATTN), row))
    kt_out = lambda dt: (jax.ShapeDtypeStruct((m // p, D_ATTN, p), dt),
                         pl.BlockSpec((1, D_ATTN, tm), kt_map))
    outs = [rows_out(BF16), kt_out(F32), kt_out(BF16), rows_out(F32), rows_out(BF16),
            rows_out(BF16), rows_out(F32), rows_out(BF16)]
    pipelined = (_nbytes((tm, D_MODEL), F32) + 3 * _nbytes((tm, LANES), F32)
                 + sum(_nbytes((tm, D_ATTN), o[0].dtype) for o in outs))
    resident = (_nbytes((D_MODEL, D_IN), BF16) + _nbytes((tm, D_MODEL), BF16)
                + 4 * _nbytes((tm, D_ATTN), F32))
    return pl.pallas_call(
        _proj_kernel,
        grid=(m // tm,),
        in_specs=in_specs,
        out_specs=tuple(o[1] for o in outs),
        out_shape=tuple(o[0] for o in outs),
        scratch_shapes=[pltpu.VMEM((tm, D_MODEL), BF16), pltpu.VMEM((tm, D_ATTN), F32)],
        compiler_params=pltpu.CompilerParams(
            dimension_semantics=("arbitrary",),
            vmem_limit_bytes=_vmem_limit(pipelined, resident)),
        name="proj",
    )(x2d, g_norm.astype(F32)[None, :], w_in_bf, bd, gq, gkt, cos_t, s1_t, s2_t,
      cos_tt, sin_tt, b_glu.astype(F32)[None, :])


def _lambda_full(lq1_ref, lk1_ref, lq2_ref, lk2_ref, lam_init):
    a = jnp.sum(lq1_ref[...] * lk1_ref[...], axis=-1, keepdims=True)
    b = jnp.sum(lq2_ref[...] * lk2_ref[...], axis=-1, keepdims=True)
    return jnp.exp(a) - jnp.exp(b) + lam_init


def _subnorm_gate(o, gsub, gate, lam_init):
    ms = jnp.mean(o * o, axis=-1, keepdims=True)
    y = o * lax.rsqrt(ms + 1e-5) * gsub
    return y * (1.0 - lam_init) * gate


def _prompt_tile(qi, lam, gsub_ref, q_ref, kt_ref, v_ref, ga_ref, o_ref, m_scr, l_scr, acc_scr, s_scr,
                 *, tq, tk, hp, lam_init):
    lane = lax.broadcasted_iota(jnp.int32, (tq, LANES), 1)
    chains = []
    for h in range(hp):
        qh = q_ref[0, :, h * LANES:(h + 1) * LANES]
        zero = jnp.zeros_like(qh)
        chains.append((h, jnp.where(lane < QK_DIM, qh, zero)))
        chains.append((h, jnp.where(lane >= QK_DIM, qh, zero)))
    m_scr[...] = jnp.full(m_scr.shape, -jnp.inf, F32)
    l_scr[...] = jnp.zeros(l_scr.shape, F32)
    acc_scr[...] = jnp.zeros(acc_scr.shape, F32)
    n_lane_tiles = tk // LANES

    def scores(c, j):
        h, qc = chains[c]
        start = pl.multiple_of(j * tk, tk)
        kt = kt_ref[0, h * LANES:(h + 1) * LANES, pl.ds(start, tk)]
        return jnp.dot(qc, kt, preferred_element_type=F32)

    def step(j, masked, prefetch):
        start = pl.multiple_of(j * tk, tk)
        if masked:
            row = qi * tq + lax.broadcasted_iota(jnp.int32, (tq, tk), 0)
            col = start + lax.broadcasted_iota(jnp.int32, (tq, tk), 1)
            visible = col <= row
        for c, (h, _) in enumerate(chains):
            vt = v_ref[0, pl.ds(start, tk), h * LANES:(h + 1) * LANES]
            s = s_scr[c]
            if prefetch:
                s_scr[c] = scores(c, j + 1)
            if masked:
                s = jnp.where(visible, s, -jnp.inf)
            tiles = [s[:, t * LANES:(t + 1) * LANES] for t in range(n_lane_tiles)]
            tile_max = functools.reduce(jnp.maximum, tiles)
            m_old = m_scr[c]
            m_new = jnp.maximum(m_old, jnp.max(tile_max, axis=1, keepdims=True))
            alpha = jnp.exp2(m_old - m_new)
            ps = [jnp.exp2(t - m_new) for t in tiles]
            l_scr[c] = alpha * l_scr[c] + functools.reduce(jnp.add, ps)
            pb = jnp.concatenate([p.astype(BF16) for p in ps], axis=1)
            acc_scr[c] = alpha * acc_scr[c] + jnp.dot(pb, vt, preferred_element_type=F32)
            m_scr[c] = m_new

    def body(j, carry):
        step(j, False, True)
        return carry

    tiles_per_q = tq // tk
    for c in range(len(chains)):
        s_scr[c] = scores(c, 0)
    lax.fori_loop(0, qi * tiles_per_q, body, 0)
    for jj in range(tiles_per_q):
        step(qi * tiles_per_q + jj, True, jj + 1 < tiles_per_q)

    for h in range(hp):
        o1 = acc_scr[2 * h] / jnp.sum(l_scr[2 * h], axis=1, keepdims=True)
        o2 = acc_scr[2 * h + 1] / jnp.sum(l_scr[2 * h + 1], axis=1, keepdims=True)
        gate = ga_ref[0, :, h * LANES:(h + 1) * LANES].astype(F32)
        y = _subnorm_gate(o1 - lam * o2, gsub_ref[...], gate, lam_init)
        o_ref[0, :, h * LANES:(h + 1) * LANES] = y.astype(BF16)


def _decode_slice(seq, s_idx, n_steps, lam, gsub_ref, qc_ref, ktn_ref, vn_ref, ga_ref, k_refs, v_refs,
                  o_ref, m_scr, l_scr, acc_scr, *, n_q, lam_init):
    n_pages_step = len(k_refs)
    rows_pass = HEADS_PER_PASS * n_q
    rows_h = 2 * n_q

    @pl.when(s_idx == 0)
    def _():
        m_scr[...] = jnp.full(m_scr.shape, -jnp.inf, F32)
        l_scr[...] = jnp.zeros(l_scr.shape, F32)
        acc_scr[...] = jnp.zeros(acc_scr.shape, F32)

    def scores(kt):
        return jnp.concatenate(
            [jnp.dot(qc_ref[0, c * rows_pass:(c + 1) * rows_pass, :],
                     kt[c * MXU_DEPTH:(c + 1) * MXU_DEPTH, :], preferred_element_type=F32)
             for c in range(N_PASSES)], axis=0)

    def update(s, values_of_head, l_lanes):
        m_old = m_scr[...]
        m_new = jnp.maximum(m_old, jnp.max(s, axis=1, keepdims=True))
        alpha = jnp.exp2(m_old - m_new)
        p = jnp.exp2(s - m_new)
        p_sum = p[:, 0:l_lanes]
        for t in range(1, s.shape[1] // l_lanes):
            p_sum = p_sum + p[:, t * l_lanes:(t + 1) * l_lanes]
        if l_lanes < LANES:
            p_sum = jnp.concatenate([p_sum, jnp.zeros((s.shape[0], LANES - l_lanes), F32)], axis=1)
        l_scr[...] = alpha * l_scr[...] + p_sum
        pb = p.astype(BF16)
        blocks = []
        for h0 in range(0, ATTN_HEADS, 2):
            w = jnp.concatenate([values_of_head(h0), values_of_head(h0 + 1)], axis=1)
            r = jnp.dot(pb[h0 * rows_h:(h0 + 2) * rows_h, :], w, preferred_element_type=F32)
            blocks += [r[:rows_h, :V_DIM], r[rows_h:, V_DIM:]]
        pv = jnp.concatenate(blocks, axis=0)
        acc_scr[...] = alpha * acc_scr[...] + pv
        m_scr[...] = m_new

    s_pages = scores(jnp.concatenate([k_refs[g][0].astype(BF16) for g in range(n_pages_step)],
                                     axis=1))

    def cached_values(h):
        return jnp.concatenate(
            [v_refs[g][0, pl.ds(h, PAGE_SIZE, stride=ATTN_HEADS), :].astype(BF16)
             for g in range(n_pages_step)], axis=0)

    update(s_pages, cached_values, LANES)

    @pl.when(s_idx == n_steps - 1)
    def _():
        s = scores(ktn_ref[...])
        n_new = s.shape[1]
        tok = lax.broadcasted_iota(jnp.int32, s.shape, 1)
        qry = lax.broadcasted_iota(jnp.int32, s.shape, 0) % n_q
        lo = seq * n_q
        visible = (tok >= lo) & (tok <= lo + qry)
        update(jnp.where(visible, s, -jnp.inf),
               lambda h: vn_ref[:, h * V_DIM:(h + 1) * V_DIM], min(n_new, LANES))

        o_all = acc_scr[...] / jnp.sum(l_scr[...], axis=1, keepdims=True)
        for h in range(ATTN_HEADS):
            o = (o_all[h * rows_h:h * rows_h + n_q]
                 - lam * o_all[h * rows_h + n_q:(h + 1) * rows_h])
            gate = ga_ref[:, h * V_DIM:(h + 1) * V_DIM]
            o_ref[:, h * V_DIM:(h + 1) * V_DIM] = _subnorm_gate(o, gsub_ref[...], gate, lam_init)


def _attn_kernel(pt_ref, lq1_ref, lk1_ref, lq2_ref, lk2_ref, gsub_ref,
                 q_ref, kt_ref, v_ref, gap_ref, qc_ref, ktn_ref, vn_ref, gas_ref, *refs,
                 page_counts, n_prompt_steps, n_decode_steps, nq, steps_per_seq,
                 tq, tk, hp, n_q, lam_init):
    del pt_ref
    g_max = max(page_counts)
    k_refs = refs[:g_max]
    v_refs = refs[g_max:2 * g_max]
    op_ref, od_ref, pm_scr, pl_scr, pacc_scr, ps_scr, dm_scr, dl_scr, dacc_scr = refs[2 * g_max:]
    step = pl.program_id(0)
    lam = _lambda_full(lq1_ref, lk1_ref, lq2_ref, lk2_ref, lam_init)
    period = len(page_counts)

    def prompt():
        _prompt_tile(step % nq, lam, gsub_ref, q_ref, kt_ref, v_ref, gap_ref, op_ref,
                     pm_scr, pl_scr, pacc_scr, ps_scr, tq=tq, tk=tk, hp=hp, lam_init=lam_init)

    def decode(n_pages_step):
        _decode_slice(step // steps_per_seq, step % steps_per_seq, steps_per_seq, lam, gsub_ref,
                      qc_ref, ktn_ref, vn_ref, gas_ref, k_refs[:n_pages_step], v_refs[:n_pages_step],
                      od_ref, dm_scr, dl_scr, dacc_scr, n_q=n_q, lam_init=lam_init)

    n_steps = max(n_prompt_steps, n_decode_steps)
    prompt() if n_prompt_steps == n_steps else pl.when(step < n_prompt_steps)(prompt)
    for phase, count in enumerate(page_counts):
        live = step < n_decode_steps
        if period > 1:
            live = live & (step % period == phase)
        if period == 1 and n_decode_steps == n_steps:
            decode(count)
        else:
            pl.when(live)(functools.partial(decode, count))


def _page_schedule(n_pages, nq):
    if nq == len(DEC_PAGE_CYCLE) and n_pages % sum(DEC_PAGE_CYCLE) == 0:
        return DEC_PAGE_CYCLE
    g = min(DEC_PAGES, n_pages)
    assert n_pages % g == 0
    return (g,)


def _page_index_table(page_table, counts, steps_per_seq):
    bd, n_pages = page_table.shape
    g_max, period = max(counts), len(counts)
    flat, prev = [], [0] * g_max
    for seq in range(bd):
        off = 0
        for k in range(steps_per_seq):
            n = counts[k % period]
            row = [seq * n_pages + off + gi if gi < n else prev[gi] for gi in range(g_max)]
            flat.append(row)
            prev, off = row, off + n
        assert off == n_pages
    return page_table.reshape(-1)[jnp.asarray(flat, jnp.int32)]


def _attention(q_p, kbt_p, vb_p, ga_p, q_s, ktn, vn, ga_s, cache_kt, cache_v, page_table,
               lams, g_sub, lam_init):
    b, t, _ = q_p.shape
    tq = min(ATTN_ROWS, t)
    tk = min(ATTN_KEYS, tq)
    hp = ATTN_HEADS_PER_STEP
    assert t % tq == 0 and tq % tk == 0 and ATTN_HEADS % hp == 0
    nq, nh = t // tq, ATTN_HEADS // hp
    n_prompt_steps = b * nh * nq
    width = hp * LANES

    bd, s, _ = q_s.shape
    n_pages = page_table.shape[1]
    assert s == SUBLANES
    counts = _page_schedule(n_pages, nq)
    g_max = max(counts)
    steps_per_seq = (n_pages // sum(counts)) * len(counts)
    n_decode_steps = bd * steps_per_seq
    page_idx = _page_index_table(page_table, counts, steps_per_seq)
    n_rows = N_QK_HEADS * s
    q6 = q_s.reshape(bd, s, N_PASSES, HEADS_PER_PASS, QK_DIM)
    eye = jnp.eye(HEADS_PER_PASS, dtype=q_s.dtype)
    qc = jnp.einsum('bqchd,hg->bchqgd', q6, eye).reshape(bd, n_rows, MXU_DEPTH)

    def prompt_idx(i):
        i = jnp.minimum(i, n_prompt_steps - 1)
        return i // (nh * nq), (i // nq) % nh, i % nq

    def decode_step(i):
        return jnp.minimum(i, n_decode_steps - 1)

    def p_tile(i, pt):
        bi, h, qi = prompt_idx(i)
        return bi, qi, h

    def p_keys(i, pt):
        bi, h, _ = prompt_idx(i)
        return bi, h, 0

    def p_vals(i, pt):
        bi, h, _ = prompt_idx(i)
        return bi, 0, h

    def page_spec(gi):
        return pl.BlockSpec((1, D_ATTN, PAGE_SIZE), lambda i, pt: (pt[decode_step(i), gi], 0, 0))

    const2 = lambda shape: pl.BlockSpec(shape, lambda i, pt: (0, 0))
    small = const2((1, QK_DIM))
    tile = pl.BlockSpec((1, tq, width), p_tile)
    per_seq = pl.BlockSpec((s, D_ATTN), lambda i, pt: (decode_step(i) // steps_per_seq, 0))
    in_specs = ([small, small, small, small, const2((1, V_DIM)),
                 tile, pl.BlockSpec((1, width, t), p_keys), pl.BlockSpec((1, t, width), p_vals), tile,
                 pl.BlockSpec((1, n_rows, MXU_DEPTH),
                              lambda i, pt: (decode_step(i) // steps_per_seq, 0, 0)),
                 const2((D_ATTN, bd * s)), const2((bd * s, D_ATTN)), per_seq]
                + [page_spec(gi) for gi in range(g_max)] + [page_spec(gi) for gi in range(g_max)])
    p_state = (2 * hp, tq, LANES)
    p_scores = (2 * hp, tq, tk)
    pipelined = (3 * _nbytes((tq, width), BF16) + 2 * _nbytes((t, width), BF16)
                 + 2 * g_max * _nbytes((PAGE_SIZE, D_ATTN), F32) + _nbytes((n_rows, MXU_DEPTH), BF16)
                 + 2 * _nbytes((bd * s, D_ATTN), BF16) + 2 * _nbytes((s, D_ATTN), F32))
    resident = (3 * _nbytes(p_state, F32) + 3 * _nbytes(p_scores, F32)
                + 3 * _nbytes((n_rows, LANES), F32) + 3 * _nbytes((n_rows, g_max * PAGE_SIZE), F32))
    grid_spec = pltpu.PrefetchScalarGridSpec(
        num_scalar_prefetch=1,
        grid=(max(n_prompt_steps, n_decode_steps),),
        in_specs=in_specs,
        out_specs=(tile, per_seq),
        scratch_shapes=[pltpu.VMEM(p_state, F32), pltpu.VMEM(p_state, F32), pltpu.VMEM(p_state, F32),
                        pltpu.VMEM(p_scores, F32),
                        pltpu.VMEM((n_rows, 1), F32), pltpu.VMEM((n_rows, LANES), F32),
                        pltpu.VMEM((n_rows, V_DIM), F32)],
    )
    return pl.pallas_call(
        functools.partial(_attn_kernel, page_counts=counts, n_prompt_steps=n_prompt_steps,
                          n_decode_steps=n_decode_steps, nq=nq, steps_per_seq=steps_per_seq,
                          tq=tq, tk=tk, hp=hp, n_q=s, lam_init=lam_init),
        grid_spec=grid_spec,
        out_shape=(jax.ShapeDtypeStruct((b, t, D_ATTN), BF16),
                   jax.ShapeDtypeStruct((bd * s, D_ATTN), F32)),
        compiler_params=pltpu.CompilerParams(
            dimension_semantics=("arbitrary",),
            vmem_limit_bytes=_vmem_limit(pipelined, resident)),
        name="attention",
    )(page_idx, *lams, g_sub.astype(F32)[None, :], q_p, kbt_p, vb_p, ga_p, qc, ktn, vn, ga_s,
      *([cache_kt] * g_max), *([cache_v] * g_max))


def _tail_kernel(ctx_ref, c_ref, gc_ref, ma_ref, x_ref, wdw_ref, bdw_ref, gln_ref, bln_ref,
                 wpw_ref, bpw_ref, wout_ref, y_ref, ext_scr, sh_scr, cv_scr, *, bs, tt, rc, zero_first):
    i = pl.program_id(1)
    sh_rows = tt + CTX_ROWS - SUBLANES
    for b in range(bs):
        ctx = ctx_ref[b]
        if zero_first:
            ctx = jnp.where(i == 0, jnp.zeros_like(ctx), ctx)
        ext_scr[0:CTX_ROWS, :] = ctx
        ext_scr[CTX_ROWS:CTX_ROWS + tt, :] = c_ref[b * tt:(b + 1) * tt, :]
        for s in range(1, SUBLANES):
            sh_scr[s - 1, 0:sh_rows, :] = ext_scr[s:s + sh_rows, :]

        def chunk(ci, carry, b=b):
            r0 = pl.multiple_of(ci * rc, rc)
            acc = jnp.broadcast_to(bdw_ref[...], (rc, D_CONV))
            for tap in range(CONV_WIDTH):
                whole, s = divmod(CTX_PAD + tap, SUBLANES)
                rows = pl.ds(r0 + whole * SUBLANES, rc)
                win = ext_scr[rows, :] if s == 0 else sh_scr[s - 1, rows, :]
                acc = acc + win * jnp.tile(wdw_ref[tap], (rc // SUBLANES, 1))
            cv_scr[pl.ds(b * tt + r0, rc), :] = acc
            return carry

        lax.fori_loop(0, tt // rc, chunk, 0)

    cv = cv_scr[...]
    mu = jnp.mean(cv, axis=-1, keepdims=True)
    d = cv - mu
    var = jnp.mean(d * d, axis=-1, keepdims=True)
    yn = d * lax.rsqrt(var + 1e-5) * gln_ref[...] + bln_ref[...]
    act = _silu(yn).astype(BF16)
    pw = jnp.dot(act, wpw_ref[...], preferred_element_type=F32) + bpw_ref[...]
    mix_c = (pw * gc_ref[...].astype(F32)).astype(BF16)
    mix_a = ma_ref[...].astype(BF16)
    y_ref[...] = (x_ref[...]
                  + jnp.dot(mix_a, wout_ref[0:D_ATTN, :], preferred_element_type=F32)
                  + jnp.dot(mix_c, wout_ref[D_ATTN:, :], preferred_element_type=F32))


def _tail(ctx, ctx_map, c, gc, mix_a, x, w_dw, b_dw, g_ln, b_ln, w_pw_bf, b_pw2, w_out_bf,
          *, b, t, bs, tt, zero_first):
    assert b % bs == 0 and t % tt == 0 and (bs == 1 or tt == t)
    rc = min(CONV_CHUNK, tt)
    assert tt % rc == 0
    tiles_per_seq = t // tt
    tile = lambda width: pl.BlockSpec((bs * tt, width), lambda bi, i: (bi * tiles_per_seq + i, 0))
    const = lambda shape: _resident(shape, lambda bi, i: (0,) * len(shape))
    in_specs = [pl.BlockSpec((bs, CTX_ROWS, D_CONV), ctx_map),
                tile(D_CONV), tile(D_CONV), tile(D_ATTN), tile(D_MODEL),
                const((CONV_WIDTH, SUBLANES, D_CONV)), const((1, D_CONV)), const((1, D_CONV)),
                const((1, D_CONV)), const((D_CONV, D_CONV)), const((1, D_CONV)),
                const((D_MODEL, D_MODEL))]
    rows = bs * tt
    ext_shape = (CTX_ROWS + tt, D_CONV)
    sh_shape = (SUBLANES - 1,) + ext_shape
    pipelined = (_nbytes((bs, CTX_ROWS, D_CONV), F32) + _nbytes((rows, D_CONV), F32)
                 + _nbytes((rows, D_CONV), gc.dtype) + _nbytes((rows, D_ATTN), mix_a.dtype)
                 + 2 * _nbytes((rows, D_MODEL), F32))
    resident = (_nbytes((CONV_WIDTH, SUBLANES, D_CONV), F32) + _nbytes((D_CONV, D_CONV), BF16)
                + _nbytes((D_MODEL, D_MODEL), BF16) + _nbytes(ext_shape, F32)
                + _nbytes(sh_shape, F32) + 4 * _nbytes((rows, D_CONV), F32))
    w_taps = jnp.broadcast_to(w_dw.astype(F32)[:, None, :], (CONV_WIDTH, SUBLANES, D_CONV))
    return pl.pallas_call(
        functools.partial(_tail_kernel, bs=bs, tt=tt, rc=rc, zero_first=zero_first),
        grid=(b // bs, tiles_per_seq),
        in_specs=in_specs,
        out_specs=tile(D_MODEL),
        out_shape=jax.ShapeDtypeStruct((b * t, D_MODEL), F32),
        scratch_shapes=[pltpu.VMEM(ext_shape, F32), pltpu.VMEM(sh_shape, F32),
                        pltpu.VMEM((rows, D_CONV), F32)],
        compiler_params=pltpu.CompilerParams(
            dimension_semantics=("arbitrary", "arbitrary"),
            vmem_limit_bytes=_vmem_limit(pipelined, resident)),
        name="tail",
    )(ctx, c, gc, mix_a, x, w_taps, b_dw.astype(F32)[None, :], g_ln.astype(F32)[None, :],
      b_ln.astype(F32)[None, :], w_pw_bf, b_pw2.astype(F32)[None, :], w_out_bf)


def kernel(x_prompt, x_sample, cache_k, cache_v, state_conv, page_table, g_norm, w_in, g_q, g_k, lambda_q1, lambda_k1, lambda_q2, lambda_k2, g_sub, b_glu, w_dw, b_dw, g_ln, b_ln, w_pw2, b_pw2, w_out):
    bp, tp, _ = x_prompt.shape
    bs, ts, _ = x_sample.shape
    depth = w_in.shape[0]
    n_pool = cache_k.shape[1]
    past = page_table.shape[1] * PAGE_SIZE
    pos_p = jnp.arange(tp)
    pos_s = jnp.tile(past + jnp.arange(ts), bs)
    tt_p = min(TAIL_ROWS, tp)
    seqs = min(DEC_SEQS, bs)
    mp, ms = bp * tp, bs * ts

    yp, ys = x_prompt.reshape(mp, D_MODEL), x_sample.reshape(ms, D_MODEL)
    outs = [[] for _ in range(6)]
    for l in range(depth):
        lam_init = 0.8 - 0.6 * math.exp(-0.3 * l)
        w_in_bf = w_in[l].astype(BF16)
        w_pw_bf = w_pw2[l].astype(BF16)
        w_out_bf = w_out[l].astype(BF16)
        lams = tuple(v[l].astype(F32)[None, :] for v in (lambda_q1, lambda_k1, lambda_q2, lambda_k2))
        tail_w = (w_dw[l], b_dw[l], g_ln[l], b_ln[l], w_pw_bf, b_pw2[l], w_out_bf)
        proj_w = (g_norm[l], w_in_bf, g_q[l], g_k[l], b_glu[l])

        q_p, kt_p, kbt_p, v_p, vb_p, ga_p, c_p, gc_p = _proj(yp, pos_p, *proj_w)
        q_s, kt_s, kbt_s, v_s, vb_s, ga_s, c_s, gc_s = _proj(ys, pos_s, *proj_w)

        as3 = lambda a: a.reshape(bp, tp, D_ATTN)
        cache_kt = cache_k[l].transpose(0, 2, 3, 1).reshape(n_pool, D_ATTN, PAGE_SIZE)
        cache_vr = cache_v[l].reshape(n_pool, PAGE_SIZE * ATTN_HEADS, V_DIM)
        mix_p, mix_s = _attention(as3(q_p), kbt_p, as3(vb_p), as3(ga_p),
                                  q_s.reshape(bs, ts, D_ATTN), kbt_s[0], vb_s, ga_s.astype(F32),
                                  cache_kt, cache_vr, page_table, lams, g_sub[l], lam_init)

        c3 = as3(c_p)
        blocks_per_tile = tt_p // CTX_ROWS
        prev_block = lambda bi, i: (bi, jnp.maximum(i * blocks_per_tile - 1, 0), 0)
        yp = _tail(c3, prev_block, c_p, gc_p, mix_p.reshape(mp, D_ATTN), yp, *tail_w,
                   b=bp, t=tp, bs=1, tt=tt_p, zero_first=True)
        outs[0].append(kt_p.reshape(bp, N_QK_HEADS, QK_DIM, tp).transpose(0, 3, 1, 2))
        outs[1].append(v_p.reshape(bp, tp, ATTN_HEADS, V_DIM))
        outs[2].append(c3[:, tp - CONV_CTX:])

        ctx = jnp.pad(state_conv[l].astype(F32), ((0, 0), (CTX_PAD, 0), (0, 0)))
        ys = _tail(ctx, lambda bi, i: (bi, 0, 0), c_s, gc_s, mix_s, ys, *tail_w,
                   b=bs, t=ts, bs=seqs, tt=ts, zero_first=False)
        outs[3].append(kt_s[0].T.reshape(bs, ts, N_QK_HEADS, QK_DIM))
        outs[4].append(v_s.reshape(bs, ts, ATTN_HEADS, V_DIM))
        cs3 = c_s.reshape(bs, ts, D_CONV)
        outs[5].append(jnp.concatenate([state_conv[l].astype(F32), cs3], axis=1)[:, -CONV_CTX:])

    return ((yp.reshape(bp, tp, D_MODEL), ys.reshape(bs, ts, D_MODEL))
            + tuple(jnp.stack(o) for o in outs))
```

```python
import functools
import math

import jax
import jax.numpy as jnp
from jax import lax
from jax.experimental import pallas as pl
from jax.experimental.pallas import tpu as pltpu

D_MODEL = 2048
D_ATTN = D_MODEL // 2
D_CONV = D_MODEL - D_ATTN
ATTN_HEADS = 8
N_QK_HEADS = 2 * ATTN_HEADS
QK_DIM = D_ATTN // N_QK_HEADS
V_DIM = 2 * QK_DIM
ROT_DIM = QK_DIM // 4
ROT_HALF = ROT_DIM // 2
ROPE_THETA = 500000.0
CONV_WIDTH = 31
CONV_CTX = CONV_WIDTH - 1
PAGE_SIZE = 128
SCALE = QK_DIM ** -0.5
Q_SCALE = SCALE * math.log2(math.e)
G_Q, G_K, G_V, G_GATE_A, G_GLU_A, G_GLU_B, G_GATE_C = range(7)
D_IN = 7 * D_ATTN

LANES = 128
SUBLANES = 8
MXU_DEPTH = 256
CTX_ROWS = 32
CTX_PAD = CTX_ROWS - CONV_CTX
VMEM_INTERNAL_BYTES = 12 << 20

PROJ_ROWS = 256
ATTN_ROWS = 512
ATTN_KEYS = 512
ATTN_HEADS_PER_STEP = 1
TAIL_ROWS = 512
CONV_CHUNK = 64
DEC_PAGES = 16
DEC_SEQS = 8

HEADS_PER_PASS = MXU_DEPTH // QK_DIM
N_PASSES = N_QK_HEADS // HEADS_PER_PASS

F32 = jnp.float32
BF16 = jnp.bfloat16


def _vmem_limit(pipelined_bytes, resident_bytes):
    return int(2 * pipelined_bytes + resident_bytes + VMEM_INTERNAL_BYTES)


def _nbytes(shape, dtype):
    return math.prod(shape) * jnp.dtype(dtype).itemsize


def _sigmoid(x):
    return 1.0 / (1.0 + jnp.exp(-x))


def _silu(x):
    return x * _sigmoid(x)


def _resident(shape, index_map):
    return pl.BlockSpec(shape, index_map, pipeline_mode=pl.Buffered(1))


def _proj_kernel(x_ref, gn_ref, w_ref, bd_ref, gq_ref, gkt_ref, cos_ref, s1_ref, s2_ref,
                 cost_ref, sint_ref, bglu_ref,
                 q_ref, kt_ref, kbt_ref, v_ref, vb_ref, ga_ref, c_ref, gc_ref, h_scr, zk_scr):
    x = x_ref[...]
    ms = jnp.mean(x * x, axis=-1, keepdims=True)
    h_scr[...] = (x * lax.rsqrt(ms + 1e-6) * gn_ref[...]).astype(BF16)

    def group(g):
        return jnp.dot(h_scr[...], w_ref[:, g * D_ATTN:(g + 1) * D_ATTN],
                       preferred_element_type=F32)

    zq = group(G_Q)
    for c in range(D_ATTN // LANES):
        zc = zq[:, c * LANES:(c + 1) * LANES]
        ss = jnp.dot((zc * zc).astype(BF16), bd_ref[...], preferred_element_type=F32)
        y = zc * lax.rsqrt(ss * (1.0 / QK_DIM) + 1e-6) * gq_ref[...]
        y = (y * cos_ref[...]
             + pltpu.roll(y, LANES - ROT_HALF, 1) * s1_ref[...]
             + pltpu.roll(y, ROT_HALF, 1) * s2_ref[...])
        q_ref[:, c * LANES:(c + 1) * LANES] = (y * Q_SCALE).astype(BF16)

    zk_scr[...] = group(G_K)
    zkt = zk_scr[...].T
    for hd in range(N_QK_HEADS):
        blk = zkt[hd * QK_DIM:(hd + 1) * QK_DIM, :]
        ss = jnp.sum(blk * blk, axis=0, keepdims=True)
        y = blk * lax.rsqrt(ss * (1.0 / QK_DIM) + 1e-6) * gkt_ref[...]
        x1, x2 = y[0:ROT_HALF], y[ROT_HALF:ROT_DIM]
        cos, sin = cost_ref[...], sint_ref[...]
        y = jnp.concatenate([x1 * cos - x2 * sin, x2 * cos + x1 * sin, y[ROT_DIM:]], axis=0)
        kt_ref[0, hd * QK_DIM:(hd + 1) * QK_DIM, :] = y
        kbt_ref[0, hd * QK_DIM:(hd + 1) * QK_DIM, :] = y.astype(BF16)

    zv = group(G_V)
    v_ref[...] = zv
    vb_ref[...] = zv.astype(BF16)
    ga_ref[...] = _silu(group(G_GATE_A)).astype(BF16)
    u = group(G_GLU_A) + bglu_ref[:, :D_CONV]
    c_ref[...] = u * _sigmoid(group(G_GLU_B) + bglu_ref[:, D_CONV:])
    gc_ref[...] = _silu(group(G_GATE_C)).astype(BF16)


def _rope_tables(pos):
    inv = jnp.power(ROPE_THETA, -jnp.arange(ROT_HALF, dtype=F32) * (2.0 / ROT_DIM))
    ang = pos.astype(F32)[:, None] * inv[None, :]
    cos, sin = jnp.cos(ang), jnp.sin(ang)
    n = pos.shape[0]
    rest = QK_DIM - ROT_DIM
    zeros = jnp.zeros((n, ROT_HALF), F32)
    cos_t = jnp.concatenate([cos, cos, jnp.ones((n, rest), F32)], axis=1)
    s1_t = jnp.concatenate([-sin, zeros, jnp.zeros((n, rest), F32)], axis=1)
    s2_t = jnp.concatenate([zeros, sin, jnp.zeros((n, rest), F32)], axis=1)
    rep = LANES // QK_DIM
    rows = tuple(jnp.tile(t, (1, rep)) for t in (cos_t, s1_t, s2_t))
    return rows + (cos.T, sin.T)


def _proj(x2d, pos, g_norm, w_in_bf, g_q, g_k, b_glu):
    m = x2d.shape[0]
    p = pos.shape[0]
    tm = min(PROJ_ROWS, m)
    assert m % p == 0 and p % tm == 0
    n_pos_blocks = p // tm
    cos_t, s1_t, s2_t, cos_tt, sin_tt = _rope_tables(pos)
    head = jnp.arange(LANES) // QK_DIM
    bd = (head[:, None] == head[None, :]).astype(BF16)
    gq = jnp.tile(g_q.astype(F32), LANES // QK_DIM)[None, :]
    gkt = jnp.broadcast_to(g_k.astype(F32)[:, None], (QK_DIM, tm))

    row = lambda i: (i, 0)
    const = lambda i: (0, 0)
    tab = lambda i: (i % n_pos_blocks, 0)
    tab_t = lambda i: (0, i % n_pos_blocks)
    kt_map = lambda i: (i // n_pos_blocks, 0, i % n_pos_blocks)
    in_specs = [
        pl.BlockSpec((tm, D_MODEL), row),
        _resident((1, D_MODEL), const),
        _resident((D_MODEL, D_IN), const),
        _resident((LANES, LANES), const),
        _resident((1, LANES), const),
        _resident((QK_DIM, tm), const),
        pl.BlockSpec((tm, LANES), tab),
        pl.BlockSpec((tm, LANES), tab),
        pl.BlockSpec((tm, LANES), tab),
        pl.BlockSpec((ROT_HALF, tm), tab_t),
        pl.BlockSpec((ROT_HALF, tm), tab_t),
        _resident((1, 2 * D_CONV), const),
    ]
    rows_out = lambda dt: (jax.ShapeDtypeStruct((m, D_ATTN), dt), pl.BlockSpec((tm, D_ATTN), row))
    kt_out = lambda dt: (jax.ShapeDtypeStruct((m // p, D_ATTN, p), dt),
                         pl.BlockSpec((1, D_ATTN, tm), kt_map))
    outs = [rows_out(BF16), kt_out(F32), kt_out(BF16), rows_out(F32), rows_out(BF16),
            rows_out(BF16), rows_out(F32), rows_out(BF16)]
    pipelined = (_nbytes((tm, D_MODEL), F32) + 3 * _nbytes((tm, LANES), F32)
                 + sum(_nbytes((tm, D_ATTN), o[0].dtype) for o in outs))
    resident = (_nbytes((D_MODEL, D_IN), BF16) + _nbytes((tm, D_MODEL), BF16)
                + 4 * _nbytes((tm, D_ATTN), F32))
    return pl.pallas_call(
        _proj_kernel,
        grid=(m // tm,),
        in_specs=in_specs,
        out_specs=tuple(o[1] for o in outs),
        out_shape=tuple(o[0] for o in outs),
        scratch_shapes=[pltpu.VMEM((tm, D_MODEL), BF16), pltpu.VMEM((tm, D_ATTN), F32)],
        compiler_params=pltpu.CompilerParams(
            dimension_semantics=("arbitrary",),
            vmem_limit_bytes=_vmem_limit(pipelined, resident)),
        name="proj",
    )(x2d, g_norm.astype(F32)[None, :], w_in_bf, bd, gq, gkt, cos_t, s1_t, s2_t,
      cos_tt, sin_tt, b_glu.astype(F32)[None, :])


def _lambda_full(lq1_ref, lk1_ref, lq2_ref, lk2_ref, lam_init):
    a = jnp.sum(lq1_ref[...] * lk1_ref[...], axis=-1, keepdims=True)
    b = jnp.sum(lq2_ref[...] * lk2_ref[...], axis=-1, keepdims=True)
    return jnp.exp(a) - jnp.exp(b) + lam_init


def _subnorm_gate(o, gsub, gate, lam_init):
    ms = jnp.mean(o * o, axis=-1, keepdims=True)
    y = o * lax.rsqrt(ms + 1e-5) * gsub
    return y * (1.0 - lam_init) * gate


def _prompt_tile(qi, lam, gsub_ref, q_ref, kt_ref, v_ref, ga_ref, o_ref, m_scr, l_scr, acc_scr, s_scr,
                 *, tq, tk, hp, lam_init):
    lane = lax.broadcasted_iota(jnp.int32, (tq, LANES), 1)
    chains = []
    for h in range(hp):
        qh = q_ref[0, :, h * LANES:(h + 1) * LANES]
        zero = jnp.zeros_like(qh)
        chains.append((h, jnp.where(lane < QK_DIM, qh, zero)))
        chains.append((h, jnp.where(lane >= QK_DIM, qh, zero)))
    m_scr[...] = jnp.full(m_scr.shape, -jnp.inf, F32)
    l_scr[...] = jnp.zeros(l_scr.shape, F32)
    acc_scr[...] = jnp.zeros(acc_scr.shape, F32)
    n_lane_tiles = tk // LANES

    def scores(c, j):
        h, qc = chains[c]
        start = pl.multiple_of(j * tk, tk)
        kt = kt_ref[0, h * LANES:(h + 1) * LANES, pl.ds(start, tk)]
        return jnp.dot(qc, kt, preferred_element_type=F32)

    def step(j, masked, prefetch):
        start = pl.multiple_of(j * tk, tk)
        if masked:
            row = qi * tq + lax.broadcasted_iota(jnp.int32, (tq, tk), 0)
            col = start + lax.broadcasted_iota(jnp.int32, (tq, tk), 1)
            visible = col <= row
        for c, (h, _) in enumerate(chains):
            vt = v_ref[0, pl.ds(start, tk), h * LANES:(h + 1) * LANES]
            s = s_scr[c]
            if prefetch:
                s_scr[c] = scores(c, j + 1)
            if masked:
                s = jnp.where(visible, s, -jnp.inf)
            tiles = [s[:, t * LANES:(t + 1) * LANES] for t in range(n_lane_tiles)]
            tile_max = functools.reduce(jnp.maximum, tiles)
            m_old = m_scr[c]
            m_new = jnp.maximum(m_old, jnp.max(tile_max, axis=1, keepdims=True))
            alpha = jnp.exp2(m_old - m_new)
            ps = [jnp.exp2(t - m_new) for t in tiles]
            l_scr[c] = alpha * l_scr[c] + functools.reduce(jnp.add, ps)
            pb = jnp.concatenate([p.astype(BF16) for p in ps], axis=1)
            acc_scr[c] = alpha * acc_scr[c] + jnp.dot(pb, vt, preferred_element_type=F32)
            m_scr[c] = m_new

    def body(j, carry):
        step(j, False, True)
        return carry

    tiles_per_q = tq // tk
    for c in range(len(chains)):
        s_scr[c] = scores(c, 0)
    lax.fori_loop(0, qi * tiles_per_q, body, 0)
    for jj in range(tiles_per_q):
        step(qi * tiles_per_q + jj, True, jj + 1 < tiles_per_q)

    for h in range(hp):
        o1 = acc_scr[2 * h] / jnp.sum(l_scr[2 * h], axis=1, keepdims=True)
        o2 = acc_scr[2 * h + 1] / jnp.sum(l_scr[2 * h + 1], axis=1, keepdims=True)
        gate = ga_ref[0, :, h * LANES:(h + 1) * LANES].astype(F32)
        y = _subnorm_gate(o1 - lam * o2, gsub_ref[...], gate, lam_init)
        o_ref[0, :, h * LANES:(h + 1) * LANES] = y.astype(BF16)


def _decode_slice(seq, s_idx, n_steps, lam, gsub_ref, qc_ref, ktn_ref, vn_ref, ga_ref, k_refs, v_refs,
                  o_ref, m_scr, l_scr, acc_scr, *, n_q, lam_init):
    n_pages_step = len(k_refs)
    rows_pass = HEADS_PER_PASS * n_q
    rows_h = 2 * n_q

    @pl.when(s_idx == 0)
    def _():
        m_scr[...] = jnp.full(m_scr.shape, -jnp.inf, F32)
        l_scr[...] = jnp.zeros(l_scr.shape, F32)
        acc_scr[...] = jnp.zeros(acc_scr.shape, F32)

    def scores(kt):
        return jnp.concatenate(
            [jnp.dot(qc_ref[0, c * rows_pass:(c + 1) * rows_pass, :],
                     kt[c * MXU_DEPTH:(c + 1) * MXU_DEPTH, :], preferred_element_type=F32)
             for c in range(N_PASSES)], axis=0)

    def update(s, values_of_head, l_lanes):
        m_old = m_scr[...]
        m_new = jnp.maximum(m_old, jnp.max(s, axis=1, keepdims=True))
        alpha = jnp.exp2(m_old - m_new)
        p = jnp.exp2(s - m_new)
        p_sum = p[:, 0:l_lanes]
        for t in range(1, s.shape[1] // l_lanes):
            p_sum = p_sum + p[:, t * l_lanes:(t + 1) * l_lanes]
        if l_lanes < LANES:
            p_sum = jnp.concatenate([p_sum, jnp.zeros((s.shape[0], LANES - l_lanes), F32)], axis=1)
        l_scr[...] = alpha * l_scr[...] + p_sum
        pb = p.astype(BF16)
        blocks = []
        for h0 in range(0, ATTN_HEADS, 2):
            w = jnp.concatenate([values_of_head(h0), values_of_head(h0 + 1)], axis=1)
            r = jnp.dot(pb[h0 * rows_h:(h0 + 2) * rows_h, :], w, preferred_element_type=F32)
            blocks += [r[:rows_h, :V_DIM], r[rows_h:, V_DIM:]]
        pv = jnp.concatenate(blocks, axis=0)
        acc_scr[...] = alpha * acc_scr[...] + pv
        m_scr[...] = m_new

    s_pages = scores(jnp.concatenate([k_refs[g][0].astype(BF16) for g in range(n_pages_step)],
                                     axis=1))

    def cached_values(h):
        return jnp.concatenate(
            [v_refs[g][0, pl.ds(h, PAGE_SIZE, stride=ATTN_HEADS), :].astype(BF16)
             for g in range(n_pages_step)], axis=0)

    update(s_pages, cached_values, LANES)

    @pl.when(s_idx == n_steps - 1)
    def _():
        s = scores(ktn_ref[...])
        n_new = s.shape[1]
        tok = lax.broadcasted_iota(jnp.int32, s.shape, 1)
        qry = lax.broadcasted_iota(jnp.int32, s.shape, 0) % n_q
        lo = seq * n_q
        visible = (tok >= lo) & (tok <= lo + qry)
        update(jnp.where(visible, s, -jnp.inf),
               lambda h: vn_ref[:, h * V_DIM:(h + 1) * V_DIM], min(n_new, LANES))

        o_all = acc_scr[...] / jnp.sum(l_scr[...], axis=1, keepdims=True)
        for h in range(ATTN_HEADS):
            o = (o_all[h * rows_h:h * rows_h + n_q]
                 - lam * o_all[h * rows_h + n_q:(h + 1) * rows_h])
            gate = ga_ref[:, h * V_DIM:(h + 1) * V_DIM]
            o_ref[:, h * V_DIM:(h + 1) * V_DIM] = _subnorm_gate(o, gsub_ref[...], gate, lam_init)


def _attn_kernel(pt_ref, lq1_ref, lk1_ref, lq2_ref, lk2_ref, gsub_ref,
                 q_ref, kt_ref, v_ref, gap_ref, qc_ref, ktn_ref, vn_ref, gas_ref, *refs,
                 n_pages_step, n_prompt_steps, n_decode_steps, nq, steps_per_seq,
                 tq, tk, hp, n_q, lam_init):
    del pt_ref
    k_refs = refs[:n_pages_step]
    v_refs = refs[n_pages_step:2 * n_pages_step]
    op_ref, od_ref, pm_scr, pl_scr, pacc_scr, ps_scr, dm_scr, dl_scr, dacc_scr = refs[2 * n_pages_step:]
    step = pl.program_id(0)
    lam = _lambda_full(lq1_ref, lk1_ref, lq2_ref, lk2_ref, lam_init)

    def prompt():
        _prompt_tile(step % nq, lam, gsub_ref, q_ref, kt_ref, v_ref, gap_ref, op_ref,
                     pm_scr, pl_scr, pacc_scr, ps_scr, tq=tq, tk=tk, hp=hp, lam_init=lam_init)

    def decode():
        _decode_slice(step // steps_per_seq, step % steps_per_seq, steps_per_seq, lam, gsub_ref,
                      qc_ref, ktn_ref, vn_ref, gas_ref, k_refs, v_refs, od_ref,
                      dm_scr, dl_scr, dacc_scr, n_q=n_q, lam_init=lam_init)

    n_steps = max(n_prompt_steps, n_decode_steps)
    prompt() if n_prompt_steps == n_steps else pl.when(step < n_prompt_steps)(prompt)
    decode() if n_decode_steps == n_steps else pl.when(step < n_decode_steps)(decode)


def _attention(q_p, kbt_p, vb_p, ga_p, q_s, ktn, vn, ga_s, cache_kt, cache_v, page_table,
               lams, g_sub, lam_init):
    b, t, _ = q_p.shape
    tq = min(ATTN_ROWS, t)
    tk = min(ATTN_KEYS, tq)
    hp = ATTN_HEADS_PER_STEP
    assert t % tq == 0 and tq % tk == 0 and ATTN_HEADS % hp == 0
    nq, nh = t // tq, ATTN_HEADS // hp
    n_prompt_steps = b * nh * nq
    width = hp * LANES

    bd, s, _ = q_s.shape
    n_pages = page_table.shape[1]
    g = min(DEC_PAGES, n_pages)
    assert n_pages % g == 0 and s == SUBLANES
    steps_per_seq = n_pages // g
    n_decode_steps = bd * steps_per_seq
    n_rows = N_QK_HEADS * s
    q6 = q_s.reshape(bd, s, N_PASSES, HEADS_PER_PASS, QK_DIM)
    eye = jnp.eye(HEADS_PER_PASS, dtype=q_s.dtype)
    qc = jnp.einsum('bqchd,hg->bchqgd', q6, eye).reshape(bd, n_rows, MXU_DEPTH)

    def prompt_idx(i):
        i = jnp.minimum(i, n_prompt_steps - 1)
        return i // (nh * nq), (i // nq) % nh, i % nq

    def decode_idx(i):
        i = jnp.minimum(i, n_decode_steps - 1)
        return i // steps_per_seq, i % steps_per_seq

    def p_tile(i, pt):
        bi, h, qi = prompt_idx(i)
        return bi, qi, h

    def p_keys(i, pt):
        bi, h, _ = prompt_idx(i)
        return bi, h, 0

    def p_vals(i, pt):
        bi, h, _ = prompt_idx(i)
        return bi, 0, h

    def page_spec(gi):
        def idx(i, pt):
            seq, grp = decode_idx(i)
            return pt[seq, grp * g + gi], 0, 0
        return pl.BlockSpec((1, D_ATTN, PAGE_SIZE), idx)

    const2 = lambda shape: pl.BlockSpec(shape, lambda i, pt: (0, 0))
    small = const2((1, QK_DIM))
    tile = pl.BlockSpec((1, tq, width), p_tile)
    per_seq = pl.BlockSpec((s, D_ATTN), lambda i, pt: (decode_idx(i)[0], 0))
    in_specs = ([small, small, small, small, const2((1, V_DIM)),
                 tile, pl.BlockSpec((1, width, t), p_keys), pl.BlockSpec((1, t, width), p_vals), tile,
                 pl.BlockSpec((1, n_rows, MXU_DEPTH), lambda i, pt: (decode_idx(i)[0], 0, 0)),
                 const2((D_ATTN, bd * s)), const2((bd * s, D_ATTN)), per_seq]
                + [page_spec(gi) for gi in range(g)] + [page_spec(gi) for gi in range(g)])
    p_state = (2 * hp, tq, LANES)
    p_scores = (2 * hp, tq, tk)
    pipelined = (3 * _nbytes((tq, width), BF16) + 2 * _nbytes((t, width), BF16)
                 + 2 * g * _nbytes((PAGE_SIZE, D_ATTN), F32) + _nbytes((n_rows, MXU_DEPTH), BF16)
                 + 2 * _nbytes((bd * s, D_ATTN), BF16) + 2 * _nbytes((s, D_ATTN), F32))
    resident = (3 * _nbytes(p_state, F32) + 3 * _nbytes(p_scores, F32)
                + 2 * _nbytes((g * PAGE_SIZE, D_ATTN), BF16) + 3 * _nbytes((n_rows, LANES), F32)
                + 4 * _nbytes((n_rows, g * PAGE_SIZE), F32))
    grid_spec = pltpu.PrefetchScalarGridSpec(
        num_scalar_prefetch=1,
        grid=(max(n_prompt_steps, n_decode_steps),),
        in_specs=in_specs,
        out_specs=(tile, per_seq),
        scratch_shapes=[pltpu.VMEM(p_state, F32), pltpu.VMEM(p_state, F32), pltpu.VMEM(p_state, F32),
                        pltpu.VMEM(p_scores, F32),
                        pltpu.VMEM((n_rows, 1), F32), pltpu.VMEM((n_rows, LANES), F32),
                        pltpu.VMEM((n_rows, V_DIM), F32)],
    )
    return pl.pallas_call(
        functools.partial(_attn_kernel, n_pages_step=g, n_prompt_steps=n_prompt_steps,
                          n_decode_steps=n_decode_steps, nq=nq, steps_per_seq=steps_per_seq,
                          tq=tq, tk=tk, hp=hp, n_q=s, lam_init=lam_init),
        grid_spec=grid_spec,
        out_shape=(jax.ShapeDtypeStruct((b, t, D_ATTN), BF16),
                   jax.ShapeDtypeStruct((bd * s, D_ATTN), F32)),
        compiler_params=pltpu.CompilerParams(
            dimension_semantics=("arbitrary",),
            vmem_limit_bytes=_vmem_limit(pipelined, resident)),
        name="attention",
    )(page_table, *lams, g_sub.astype(F32)[None, :], q_p, kbt_p, vb_p, ga_p, qc, ktn, vn, ga_s,
      *([cache_kt] * g), *([cache_v] * g))


def _tail_kernel(ctx_ref, c_ref, gc_ref, ma_ref, x_ref, wdw_ref, bdw_ref, gln_ref, bln_ref,
                 wpw_ref, bpw_ref, wout_ref, y_ref, ext_scr, sh_scr, cv_scr, *, bs, tt, rc, zero_first):
    i = pl.program_id(1)
    sh_rows = tt + CTX_ROWS - SUBLANES
    for b in range(bs):
        ctx = ctx_ref[b]
        if zero_first:
            ctx = jnp.where(i == 0, jnp.zeros_like(ctx), ctx)
        ext_scr[0:CTX_ROWS, :] = ctx
        ext_scr[CTX_ROWS:CTX_ROWS + tt, :] = c_ref[b * tt:(b + 1) * tt, :]
        for s in range(1, SUBLANES):
            sh_scr[s - 1, 0:sh_rows, :] = ext_scr[s:s + sh_rows, :]

        def chunk(ci, carry, b=b):
            r0 = pl.multiple_of(ci * rc, rc)
            acc = jnp.broadcast_to(bdw_ref[...], (rc, D_CONV))
            for tap in range(CONV_WIDTH):
                whole, s = divmod(CTX_PAD + tap, SUBLANES)
                rows = pl.ds(r0 + whole * SUBLANES, rc)
                win = ext_scr[rows, :] if s == 0 else sh_scr[s - 1, rows, :]
                acc = acc + win * jnp.tile(wdw_ref[tap], (rc // SUBLANES, 1))
            cv_scr[pl.ds(b * tt + r0, rc), :] = acc
            return carry

        lax.fori_loop(0, tt // rc, chunk, 0)

    cv = cv_scr[...]
    mu = jnp.mean(cv, axis=-1, keepdims=True)
    d = cv - mu
    var = jnp.mean(d * d, axis=-1, keepdims=True)
    yn = d * lax.rsqrt(var + 1e-5) * gln_ref[...] + bln_ref[...]
    act = _silu(yn).astype(BF16)
    pw = jnp.dot(act, wpw_ref[...], preferred_element_type=F32) + bpw_ref[...]
    mix_c = (pw * gc_ref[...].astype(F32)).astype(BF16)
    mix_a = ma_ref[...].astype(BF16)
    y_ref[...] = (x_ref[...]
                  + jnp.dot(mix_a, wout_ref[0:D_ATTN, :], preferred_element_type=F32)
                  + jnp.dot(mix_c, wout_ref[D_ATTN:, :], preferred_element_type=F32))


def _tail(ctx, ctx_map, c, gc, mix_a, x, w_dw, b_dw, g_ln, b_ln, w_pw_bf, b_pw2, w_out_bf,
          *, b, t, bs, tt, zero_first):
    assert b % bs == 0 and t % tt == 0 and (bs == 1 or tt == t)
    rc = min(CONV_CHUNK, tt)
    assert tt % rc == 0
    tiles_per_seq = t // tt
    tile = lambda width: pl.BlockSpec((bs * tt, width), lambda bi, i: (bi * tiles_per_seq + i, 0))
    const = lambda shape: _resident(shape, lambda bi, i: (0,) * len(shape))
    in_specs = [pl.BlockSpec((bs, CTX_ROWS, D_CONV), ctx_map),
                tile(D_CONV), tile(D_CONV), tile(D_ATTN), tile(D_MODEL),
                const((CONV_WIDTH, SUBLANES, D_CONV)), const((1, D_CONV)), const((1, D_CONV)),
                const((1, D_CONV)), const((D_CONV, D_CONV)), const((1, D_CONV)),
                const((D_MODEL, D_MODEL))]
    rows = bs * tt
    ext_shape = (CTX_ROWS + tt, D_CONV)
    sh_shape = (SUBLANES - 1,) + ext_shape
    pipelined = (_nbytes((bs, CTX_ROWS, D_CONV), F32) + _nbytes((rows, D_CONV), F32)
                 + _nbytes((rows, D_CONV), gc.dtype) + _nbytes((rows, D_ATTN), mix_a.dtype)
                 + 2 * _nbytes((rows, D_MODEL), F32))
    resident = (_nbytes((CONV_WIDTH, SUBLANES, D_CONV), F32) + _nbytes((D_CONV, D_CONV), BF16)
                + _nbytes((D_MODEL, D_MODEL), BF16) + _nbytes(ext_shape, F32)
                + _nbytes(sh_shape, F32) + 4 * _nbytes((rows, D_CONV), F32))
    w_taps = jnp.broadcast_to(w_dw.astype(F32)[:, None, :], (CONV_WIDTH, SUBLANES, D_CONV))
    return pl.pallas_call(
        functools.partial(_tail_kernel, bs=bs, tt=tt, rc=rc, zero_first=zero_first),
        grid=(b // bs, tiles_per_seq),
        in_specs=in_specs,
        out_specs=tile(D_MODEL),
        out_shape=jax.ShapeDtypeStruct((b * t, D_MODEL), F32),
        scratch_shapes=[pltpu.VMEM(ext_shape, F32), pltpu.VMEM(sh_shape, F32),
                        pltpu.VMEM((rows, D_CONV), F32)],
        compiler_params=pltpu.CompilerParams(
            dimension_semantics=("arbitrary", "arbitrary"),
            vmem_limit_bytes=_vmem_limit(pipelined, resident)),
        name="tail",
    )(ctx, c, gc, mix_a, x, w_taps, b_dw.astype(F32)[None, :], g_ln.astype(F32)[None, :],
      b_ln.astype(F32)[None, :], w_pw_bf, b_pw2.astype(F32)[None, :], w_out_bf)


def kernel(x_prompt, x_sample, cache_k, cache_v, state_conv, page_table, g_norm, w_in, g_q, g_k, lambda_q1, lambda_k1, lambda_q2, lambda_k2, g_sub, b_glu, w_dw, b_dw, g_ln, b_ln, w_pw2, b_pw2, w_out):
    bp, tp, _ = x_prompt.shape
    bs, ts, _ = x_sample.shape
    depth = w_in.shape[0]
    n_pool = cache_k.shape[1]
    past = page_table.shape[1] * PAGE_SIZE
    pos_p = jnp.arange(tp)
    pos_s = jnp.tile(past + jnp.arange(ts), bs)
    tt_p = min(TAIL_ROWS, tp)
    seqs = min(DEC_SEQS, bs)
    mp, ms = bp * tp, bs * ts

    yp, ys = x_prompt.reshape(mp, D_MODEL), x_sample.reshape(ms, D_MODEL)
    outs = [[] for _ in range(6)]
    for l in range(depth):
        lam_init = 0.8 - 0.6 * math.exp(-0.3 * l)
        w_in_bf = w_in[l].astype(BF16)
        w_pw_bf = w_pw2[l].astype(BF16)
        w_out_bf = w_out[l].astype(BF16)
        lams = tuple(v[l].astype(F32)[None, :] for v in (lambda_q1, lambda_k1, lambda_q2, lambda_k2))
        tail_w = (w_dw[l], b_dw[l], g_ln[l], b_ln[l], w_pw_bf, b_pw2[l], w_out_bf)
        proj_w = (g_norm[l], w_in_bf, g_q[l], g_k[l], b_glu[l])

        q_p, kt_p, kbt_p, v_p, vb_p, ga_p, c_p, gc_p = _proj(yp, pos_p, *proj_w)
        q_s, kt_s, kbt_s, v_s, vb_s, ga_s, c_s, gc_s = _proj(ys, pos_s, *proj_w)

        as3 = lambda a: a.reshape(bp, tp, D_ATTN)
        cache_kt = cache_k[l].transpose(0, 2, 3, 1).reshape(n_pool, D_ATTN, PAGE_SIZE)
        cache_vr = cache_v[l].reshape(n_pool, PAGE_SIZE * ATTN_HEADS, V_DIM)
        mix_p, mix_s = _attention(as3(q_p), kbt_p, as3(vb_p), as3(ga_p),
                                  q_s.reshape(bs, ts, D_ATTN), kbt_s[0], vb_s, ga_s.astype(F32),
                                  cache_kt, cache_vr, page_table, lams, g_sub[l], lam_init)

        c3 = as3(c_p)
        blocks_per_tile = tt_p // CTX_ROWS
        prev_block = lambda bi, i: (bi, jnp.maximum(i * blocks_per_tile - 1, 0), 0)
        yp = _tail(c3, prev_block, c_p, gc_p, mix_p.reshape(mp, D_ATTN), yp, *tail_w,
                   b=bp, t=tp, bs=1, tt=tt_p, zero_first=True)
        outs[0].append(kt_p.reshape(bp, N_QK_HEADS, QK_DIM, tp).transpose(0, 3, 1, 2))
        outs[1].append(v_p.reshape(bp, tp, ATTN_HEADS, V_DIM))
        outs[2].append(c3[:, tp - CONV_CTX:])

        ctx = jnp.pad(state_conv[l].astype(F32), ((0, 0), (CTX_PAD, 0), (0, 0)))
        ys = _tail(ctx, lambda bi, i: (bi, 0, 0), c_s, gc_s, mix_s, ys, *tail_w,
                   b=bs, t=ts, bs=seqs, tt=ts, zero_first=False)
        outs[3].append(kt_s[0].T.reshape(bs, ts, N_QK_HEADS, QK_DIM))
        outs[4].append(v_s.reshape(bs, ts, ATTN_HEADS, V_DIM))
        cs3 = c_s.reshape(bs, ts, D_CONV)
        outs[5].append(jnp.concatenate([state_conv[l].astype(F32), cs3], axis=1)[:, -CONV_CTX:])

    return ((yp.reshape(bp, tp, D_MODEL), ys.reshape(bs, ts, D_MODEL))
            + tuple(jnp.stack(o) for o in outs))
```
